```python
import jax, jax.numpy as jnp
from jax import lax
import numpy as np

D_MODEL = 1024
BATCH = 8
SEQ = 4096
DEPTH = 1

HEAD_DIM = 64
RET_HEADS = 8
DSA_HEADS = 8
RET_WIDTH = RET_HEADS * HEAD_DIM
DSA_WIDTH = DSA_HEADS * HEAD_DIM
MIX_WIDTH = RET_WIDTH + DSA_WIDTH
IDX_HEADS = 8
IDX_DIM = 64
TOPK_MAX = 256
RET_CHUNK = 128
Q_BLOCK = 128
D_FF = 2816
CONV_WIDTH = 3
ROPE_THETA = 10000.0
EPS = 1e-6
IN_SIZES = (RET_WIDTH, RET_WIDTH, RET_WIDTH, RET_WIDTH,
            DSA_WIDTH, HEAD_DIM, HEAD_DIM,
            IDX_HEADS * IDX_DIM, IDX_DIM, IDX_HEADS)
IN_WIDTH = 4 * RET_WIDTH + DSA_WIDTH + 2 * HEAD_DIM + IDX_HEADS * IDX_DIM + IDX_DIM + IDX_HEADS

kernel_name = "hymba_retention_dsa_convglu_sandwich"


def rms_norm(x, g):
    xf = x.astype(jnp.float32)
    y = xf * lax.rsqrt(jnp.mean(xf * xf, axis=-1, keepdims=True) + EPS)
    return (y * g.astype(jnp.float32)).astype(x.dtype)


def rope_tables(seq_len, dim):
    inv = ROPE_THETA ** (-jnp.arange(0, dim, 2, dtype=jnp.float32) / dim)
    ang = jnp.arange(seq_len, dtype=jnp.float32)[:, None] * inv[None, :]
    return jnp.cos(ang), jnp.sin(ang)


def apply_rope(x, cos, sin):
    shape = (cos.shape[0],) + (1,) * (x.ndim - 3) + (cos.shape[1],)
    c = cos.reshape(shape)
    s = sin.reshape(shape)
    xf = x.astype(jnp.float32)
    x1, x2 = jnp.split(xf, 2, axis=-1)
    return jnp.concatenate([x1 * c - x2 * s, x2 * c + x1 * s], axis=-1).astype(x.dtype)


def split_columns(p):
    pts, acc = [], 0
    for n in IN_SIZES[:-1]:
        acc += n
        pts.append(acc)
    return jnp.split(p, pts, axis=-1)


def retention(q, k, v, gate):
    B, S, H, d = q.shape
    C = RET_CHUNK
    N = S // C
    f32 = jnp.float32
    log_g = jnp.log(1.0 - 2.0 ** (-5.0 - jnp.arange(H, dtype=f32)))
    qf = q.astype(f32).reshape(B, N, C, H, d)
    kf = (k.astype(f32) * d ** -0.5).reshape(B, N, C, H, d)
    vf = v.astype(f32).reshape(B, N, C, H, d)
    pos = jnp.arange(C, dtype=f32)
    diff = pos[:, None] - pos[None, :]
    decay_in = jnp.where(diff >= 0, jnp.exp(log_g[:, None, None] * jnp.maximum(diff, 0.0)), 0.0)
    scores = jnp.einsum('bnchd,bnmhd->bnhcm', qf, kf) * decay_in[None, None]
    inner = jnp.einsum('bnhcm,bnmhd->bnchd', scores, vf)
    zeta = jnp.exp(log_g[:, None] * (C - 1.0 - pos)[None, :])
    chunk_state = jnp.einsum('bnmhk,hm,bnmhv->bnhkv', kf, zeta, vf)
    chunk_decay = jnp.exp(log_g * C)[None, :, None, None]

    def step(R, s):
        return chunk_decay * R + s, R

    _, prev = lax.scan(step, jnp.zeros((B, H, d, d), f32), jnp.moveaxis(chunk_state, 1, 0))
    prev = jnp.moveaxis(prev, 0, 1)
    xi = jnp.exp(log_g[:, None] * (pos + 1.0)[None, :])
    cross = jnp.einsum('bnchk,bnhkv,hc->bnchv', qf, prev, xi)
    o = (inner + cross).reshape(B, S, H, d)
    o = o * lax.rsqrt(jnp.mean(o * o, axis=-1, keepdims=True) + EPS)
    o = o * jax.nn.silu(gate.astype(f32).reshape(B, S, H, d))
    return o.reshape(B, S, H * d).astype(q.dtype)


def dsa_attention(q, k, v, qi, ki, wi):
    B, S, H, d = q.shape
    k_sel = min(TOPK_MAX, S // 4)
    NB = S // Q_BLOCK
    wi = wi * (IDX_HEADS ** -0.5 * IDX_DIM ** -0.5)

    def to_blocks(a):
        return jnp.moveaxis(a.reshape((B, NB, Q_BLOCK) + a.shape[2:]), 1, 0)

    key_pos = jnp.arange(S, dtype=jnp.int32)

    def block_fn(args):
        qb, qib, wib, blk = args
        t = blk * Q_BLOCK + jnp.arange(Q_BLOCK, dtype=jnp.int32)
        rel = jax.nn.relu(jnp.einsum('bthe,bse->bths', qib, ki))
        iscore = jnp.einsum('bths,bth->bts', rel, wib).astype(jnp.float32)
        causal = key_pos[None, :] <= t[:, None]
        iscore = jnp.where(causal[None], iscore, -jnp.inf)
        _, sel = lax.top_k(iscore, k_sel)
        valid = sel <= t[None, :, None]
        k_g = jax.vmap(lambda kb, ib: kb[ib])(k, sel)
        v_g = jax.vmap(lambda vb, ib: vb[ib])(v, sel)
        logits = jnp.einsum('bthd,btkd->bhtk', qb, k_g).astype(jnp.float32) * d ** -0.5
        logits = jnp.where(valid[:, None], logits, -1e30)
        p = jax.nn.softmax(logits, axis=-1).astype(v.dtype)
        return jnp.einsum('bhtk,btkd->bthd', p, v_g)

    out = lax.map(block_fn, (to_blocks(q), to_blocks(qi), to_blocks(wi),
                             jnp.arange(NB, dtype=jnp.int32)))
    return jnp.moveaxis(out, 0, 1).reshape(B, S, H * d)


def causal_dwconv(h, w, b):
    S = h.shape[1]
    hp = jnp.pad(h, ((0, 0), (CONV_WIDTH - 1, 0), (0, 0)))
    y = b
    for j in range(CONV_WIDTH):
        y = y + hp[:, j:j + S] * w[j]
    return y


def setup_inputs(seed: int = 0) -> dict:
    key = jax.random.key(seed)
    ks = jax.random.split(key, 11)
    f32 = jnp.float32
    nrm = lambda k, shape, scale: jax.random.normal(k, shape, f32) * scale
    gain = lambda k: 1.0 + 0.02 * jax.random.normal(k, (DEPTH, D_MODEL), f32)
    return {
        "x": jax.random.normal(ks[0], (BATCH, SEQ, D_MODEL), f32),
        "mix_norm_pre": gain(ks[1]),
        "mix_norm_post": gain(ks[2]),
        "w_in": nrm(ks[3], (DEPTH, D_MODEL, IN_WIDTH), D_MODEL ** -0.5),
        "w_out": nrm(ks[4], (DEPTH, MIX_WIDTH, D_MODEL), MIX_WIDTH ** -0.5),
        "ffn_norm_pre": gain(ks[5]),
        "ffn_norm_post": gain(ks[6]),
        "w_up": nrm(ks[7], (DEPTH, D_MODEL, 2 * D_FF), D_MODEL ** -0.5),
        "conv_w": nrm(ks[8], (DEPTH, CONV_WIDTH, 2 * D_FF), CONV_WIDTH ** -0.5),
        "conv_b": nrm(ks[9], (DEPTH, 2 * D_FF), 0.01),
        "w_down": nrm(ks[10], (DEPTH, D_FF, D_MODEL), D_FF ** -0.5),
    }


def reference(x, mix_norm_pre, mix_norm_post, w_in, w_out, ffn_norm_pre, ffn_norm_post,
              w_up, conv_w, conv_b, w_down):
    B, S, _ = x.shape
    cos_h, sin_h = rope_tables(S, HEAD_DIM)
    cos_i, sin_i = rope_tables(S, IDX_DIM)
    h = x
    for l in range(DEPTH):
        hn = rms_norm(h, mix_norm_pre[l])
        proj = hn @ w_in[l]
        rq, rk, rv, rg, aq, ak, av, iq, ik, iw = split_columns(proj)
        rq = apply_rope(rq.reshape(B, S, RET_HEADS, HEAD_DIM), cos_h, sin_h)
        rk = apply_rope(rk.reshape(B, S, RET_HEADS, HEAD_DIM), cos_h, sin_h)
        rv = rv.reshape(B, S, RET_HEADS, HEAD_DIM)
        ret_out = retention(rq, rk, rv, rg)
        aq = apply_rope(aq.reshape(B, S, DSA_HEADS, HEAD_DIM), cos_h, sin_h)
        ak = apply_rope(ak, cos_h, sin_h)
        iq = apply_rope(iq.reshape(B, S, IDX_HEADS, IDX_DIM), cos_i, sin_i)
        ik = apply_rope(ik, cos_i, sin_i)
        att_out = dsa_attention(aq, ak, av, iq, ik, iw)
        mixed = jnp.concatenate([ret_out, att_out], axis=-1) @ w_out[l]
        h = h + rms_norm(mixed, mix_norm_post[l])
        hn = rms_norm(h, ffn_norm_pre[l])
        up = causal_dwconv(hn @ w_up[l], conv_w[l], conv_b[l])
        g, u = jnp.split(up, 2, axis=-1)
        ffn = (jax.nn.silu(g) * u) @ w_down[l]
        h = h + rms_norm(ffn, ffn_norm_post[l])
    return h
```

```python
import functools

import jax
import jax.numpy as jnp
from jax import lax
from jax.experimental import pallas as pl
from jax.experimental.pallas import tpu as pltpu

HEAD_DIM = 64
RET_HEADS = 8
DSA_HEADS = 8
IDX_HEADS = 8
IDX_DIM = 64
TOPK_MAX = 256
D_FF = 2816
CONV_WIDTH = 3
ROPE_THETA = 10000.0
EPS = 1e-6

LANES = 128
PAIR = LANES // HEAD_DIM
N_PAIRS = RET_HEADS // PAIR
GROUP = RET_HEADS * HEAD_DIM

_OFF_RQ, _OFF_RK, _OFF_AQ, _OFF_IQ, _OFF_RV, _OFF_RG = (i * GROUP for i in range(6))
_OFF_KK = 6 * GROUP
_OFF_VV = _OFF_KK + LANES
_OFF_IK = _OFF_VV + LANES
_OFF_IW = _OFF_IK + LANES
_W_TOTAL = _OFF_IW + LANES

TM_PROJ = 512
RET_CHUNK = 256
TQ = 256
KC = 256
TM_FFN = 1024
TF = 256
VMEM_LIMIT = 48 * 1024 * 1024

NEG_BIG = -1e30
INT_MIN = -2 ** 31
KEY_NEG_INF = -2139095041

f32 = jnp.float32
bf16 = jnp.bfloat16


def _dot(a, b):
    return jnp.dot(a, b, preferred_element_type=f32)


def _dot_nt(a, b):
    return lax.dot_general(a, b, (((1,), (1,)), ((), ())), preferred_element_type=f32)


def _dot_tn(a, b):
    return lax.dot_general(a, b, (((0,), (0,)), ((), ())), preferred_element_type=f32)


def _rms(x, g):
    return x * lax.rsqrt(jnp.mean(x * x, axis=-1, keepdims=True) + EPS) * g


def _silu(x):
    return x * (1.0 / (1.0 + jnp.exp(-x)))


def _inproj_kernel(x_ref, g_ref, w_ref, ch_ref, sh_ref,
                   rq_ref, rk_ref, aq_ref, iq_ref, rv_ref, rg_ref, kk_ref, vv_ref, ik_ref, iw_ref):
    tm = x_ref.shape[0]
    hn = _rms(x_ref[...], g_ref[...]).astype(bf16)
    cos = ch_ref[...]
    sin = sh_ref[...]
    lane = lax.broadcasted_iota(jnp.int32, (tm, LANES), 1)
    first_half = (lane % HEAD_DIM) < (HEAD_DIM // 2)

    def proj(off, width):
        return _dot(hn, w_ref[:, off:off + width])

    def rope(p):
        swapped = jnp.where(first_half, pltpu.roll(p, LANES - HEAD_DIM // 2, 1),
                            pltpu.roll(p, HEAD_DIM // 2, 1))
        return p * cos + swapped * sin

    def roped_group(off, out_ref, scale):
        p = proj(off, GROUP)
        for b in range(GROUP // LANES):
            r = rope(p[:, b * LANES:(b + 1) * LANES])
            if scale != 1.0:
                r = r * scale
            out_ref[:, b * LANES:(b + 1) * LANES] = r.astype(out_ref.dtype)

    roped_group(_OFF_RQ, rq_ref, 1.0)
    roped_group(_OFF_RK, rk_ref, HEAD_DIM ** -0.5)
    roped_group(_OFF_AQ, aq_ref, HEAD_DIM ** -0.5)
    roped_group(_OFF_IQ, iq_ref, 1.0)
    rv_ref[...] = proj(_OFF_RV, GROUP).astype(bf16)
    rg_ref[...] = proj(_OFF_RG, GROUP)
    small = proj(_OFF_KK, 4 * LANES)
    kk_ref[...] = rope(small[:, 0:LANES]).astype(bf16)
    vv_ref[...] = small[:, LANES:2 * LANES].astype(bf16)
    ik_ref[...] = rope(small[:, 2 * LANES:3 * LANES]).astype(bf16)
    iw_ref[...] = small[:, 3 * LANES:4 * LANES] * (IDX_HEADS ** -0.5 * IDX_DIM ** -0.5)


def _inproj(x2, g, w, cos_t, sin_t, seq):
    t, d = x2.shape
    tm = min(TM_PROJ, seq)
    n_seq = seq // tm
    row = lambda i: (i, 0)
    fixed = lambda i: (0, 0)
    pos = lambda i: (i % n_seq, 0)
    wide = lambda dt: jax.ShapeDtypeStruct((t, GROUP), dt)
    narrow = lambda dt: jax.ShapeDtypeStruct((t, LANES), dt)
    return pl.pallas_call(
        _inproj_kernel,
        grid=(t // tm,),
        in_specs=[pl.BlockSpec((tm, d), row), pl.BlockSpec((1, d), fixed),
                  pl.BlockSpec((d, _W_TOTAL), fixed),
                  pl.BlockSpec((tm, LANES), pos), pl.BlockSpec((tm, LANES), pos)],
        out_specs=[pl.BlockSpec((tm, GROUP), row)] * 6 + [pl.BlockSpec((tm, LANES), row)] * 4,
        out_shape=[wide(bf16), wide(bf16), wide(bf16), wide(bf16), wide(bf16), wide(f32),
                   narrow(bf16), narrow(bf16), narrow(bf16), narrow(f32)],
        compiler_params=pltpu.CompilerParams(dimension_semantics=("arbitrary",),
                                             vmem_limit_bytes=VMEM_LIMIT),
        name="in_proj",
    )(x2, g, w, cos_t, sin_t)


def _retention_kernel(rq_ref, rk_ref, rv_ref, rg_ref, decay_ref, zeta_ref, xi_ref, cd_ref,
                      out_ref, state_ref):
    c = rq_ref.shape[0]

    @pl.when(pl.program_id(1) == 0)
    def _():
        state_ref[...] = jnp.zeros_like(state_ref)

    lane = lax.broadcasted_iota(jnp.int32, (c, LANES), 1)
    even = lane < HEAD_DIM
    r_i = lax.broadcasted_iota(jnp.int32, (LANES, LANES), 0)
    c_i = lax.broadcasted_iota(jnp.int32, (LANES, LANES), 1)
    same_head = (r_i < HEAD_DIM) == (c_i < HEAD_DIM)
    ones_bd = jnp.where(same_head, 1.0, 0.0).astype(bf16)

    for p in range(N_PAIRS):
        sl = slice(p * LANES, (p + 1) * LANES)
        q2 = rq_ref[:, sl]
        k2 = rk_ref[:, sl]
        v2 = rv_ref[:, sl]
        zero = jnp.zeros_like(q2)
        s_e = _dot_nt(jnp.where(even, q2, zero), k2) * decay_ref[2 * p]
        s_o = _dot_nt(jnp.where(even, zero, q2), k2) * decay_ref[2 * p + 1]
        inner = jnp.where(even, _dot(s_e.astype(bf16), v2), _dot(s_o.astype(bf16), v2))
        r_prev = state_ref[p]
        cross = _dot(q2, r_prev.astype(bf16)) * xi_ref[p]
        o = inner + cross
        kz = (k2.astype(f32) * zeta_ref[p]).astype(bf16)
        s_new = jnp.where(same_head, _dot_tn(kz, v2), 0.0)
        state_ref[p] = cd_ref[p] * r_prev + s_new
        sq = o * o
        hi = sq.astype(bf16)
        lo = (sq - hi.astype(f32)).astype(bf16)
        ms = (_dot(hi, ones_bd) + _dot(lo, ones_bd)) * (1.0 / HEAD_DIM)
        o = o * lax.rsqrt(ms + EPS)
        out_ref[:, sl] = (o * _silu(rg_ref[:, sl])).astype(out_ref.dtype)


def _retention_tables(c):
    h = jnp.arange(RET_HEADS, dtype=f32)
    log_g = jnp.log(1.0 - 2.0 ** (-5.0 - h))
    pos = jnp.arange(c, dtype=f32)
    diff = pos[:, None] - pos[None, :]
    decay = jnp.where(diff >= 0, jnp.exp(log_g[:, None, None] * jnp.maximum(diff, 0.0)), 0.0)
    zeta = jnp.exp(log_g[:, None] * (c - 1.0 - pos)[None, :])
    xi = jnp.exp(log_g[:, None] * (pos + 1.0)[None, :])
    cd = jnp.exp(log_g * c)

    def pair_lanes(a):
        a = a.reshape(N_PAIRS, PAIR, c)
        return jnp.repeat(jnp.transpose(a, (0, 2, 1)), HEAD_DIM, axis=2)

    cd2 = jnp.repeat(cd.reshape(N_PAIRS, 1, PAIR), HEAD_DIM, axis=2)
    return decay, pair_lanes(zeta), pair_lanes(xi), cd2


def _retention(rq, rk, rv, rg, batch, seq):
    t = rq.shape[0]
    c = min(RET_CHUNK, seq)
    n = seq // c
    decay, zeta, xi, cd = _retention_tables(c)
    row = lambda b, i: (b * n + i, 0)
    fix3 = lambda b, i: (0, 0, 0)
    return pl.pallas_call(
        _retention_kernel,
        grid=(batch, n),
        in_specs=[pl.BlockSpec((c, GROUP), row)] * 4 + [
            pl.BlockSpec((RET_HEADS, c, c), fix3), pl.BlockSpec((N_PAIRS, c, LANES), fix3),
            pl.BlockSpec((N_PAIRS, c, LANES), fix3), pl.BlockSpec((N_PAIRS, 1, LANES), fix3)],
        out_specs=pl.BlockSpec((c, GROUP), row),
        out_shape=jax.ShapeDtypeStruct((t, GROUP), bf16),
        scratch_shapes=[pltpu.VMEM((N_PAIRS, LANES, LANES), f32)],
        compiler_params=pltpu.CompilerParams(dimension_semantics=("arbitrary", "arbitrary"),
                                             vmem_limit_bytes=VMEM_LIMIT),
        name="retention",
    )(rq, rk, rv, rg, decay, zeta, xi, cd)


def _dsa_kernel(k_sel, aq_ref, iq_ref, iw_ref, kk_ref, vv_ref, ik_ref, out_ref,
                lhs_a, lhs_i, wbc, keys, bias, logit, thr, cnt_thr, m_ref, l_ref, acc_ref, vze, vzo):
    tq = aq_ref.shape[0]
    kc = keys.shape[2]
    j = pl.program_id(1)
    n_chunks = (j * tq) // kc + tq // kc
    kf = float(k_sel)

    lane = lax.broadcasted_iota(jnp.int32, (tq, LANES), 1)
    even = lane < HEAD_DIM

    @pl.when(j == 0)
    def _():
        lane_s = lax.broadcasted_iota(jnp.int32, vv_ref.shape, 1)
        v = vv_ref[...]
        zero = jnp.zeros_like(v)
        vze[...] = jnp.where(lane_s < HEAD_DIM, v, zero)
        vzo[...] = jnp.where(lane_s < HEAD_DIM, zero, v)

    for p in range(N_PAIRS):
        sl = slice(p * LANES, (p + 1) * LANES)
        a2 = aq_ref[:, sl]
        i2 = iq_ref[:, sl]
        zero = jnp.zeros_like(a2)
        lhs_a[(2 * p) * tq:(2 * p + 1) * tq, :] = jnp.where(even, a2, zero)
        lhs_a[(2 * p + 1) * tq:(2 * p + 2) * tq, :] = jnp.where(even, zero, a2)
        lhs_i[(2 * p) * tq:(2 * p + 1) * tq, :] = jnp.where(even, i2, zero)
        lhs_i[(2 * p + 1) * tq:(2 * p + 2) * tq, :] = jnp.where(even, zero, i2)
    for h in range(IDX_HEADS):
        wbc[h] = jnp.broadcast_to(iw_ref[:, h:h + 1], (tq, LANES))

    row_t = j * tq + lax.broadcasted_iota(jnp.int32, (tq, LANES), 0)

    def score_chunk(c, _):
        start = pl.multiple_of(c * kc, kc)
        rel = _dot_nt(lhs_i[...], ik_ref[pl.ds(start, kc), :])
        for half in range(kc // LANES):
            hs = slice(half * LANES, (half + 1) * LANES)
            isc = jnp.zeros((tq, LANES), f32)
            for h in range(IDX_HEADS):
                isc = isc + wbc[h] * jnp.maximum(rel[h * tq:(h + 1) * tq, hs], 0.0)
            key_pos = c * kc + half * LANES + lane
            isc = jnp.where(key_pos <= row_t, isc, -jnp.inf)
            bits = lax.bitcast_convert_type(isc, jnp.int32)
            key = jnp.where(bits < 0, bits ^ jnp.int32(0x7FFFFFFF), bits)
            key = jnp.where(isc == 0.0, 0, key)
            keys[c, :, hs] = key
        return 0

    lax.fori_loop(0, n_chunks, score_chunk, 0)

    thr[...] = jnp.full((tq, LANES), INT_MIN, jnp.int32)
    cnt_thr[...] = jnp.full((tq, LANES), 0.0, f32) + (n_chunks * kc).astype(f32)

    def count_ge(cand):
        def body(c, acc):
            for half in range(kc // LANES):
                hs = slice(half * LANES, (half + 1) * LANES)
                acc = acc + jnp.where(keys[c, :, hs] >= cand, 1.0, 0.0)
            return acc
        acc = lax.fori_loop(0, n_chunks, body, jnp.zeros((tq, LANES), f32))
        return jnp.broadcast_to(jnp.sum(acc, axis=1, keepdims=True), (tq, LANES))

    def bit_step(i, _):
        step = lax.shift_left(jnp.int32(1), jnp.int32(31) - i)
        cand = thr[...] + step
        cnt = count_ge(cand)
        ok = cnt >= kf
        thr[...] = jnp.where(ok, cand, thr[...])
        cnt_thr[...] = jnp.where(ok, cnt, cnt_thr[...])
        return 0

    lax.fori_loop(0, 32, bit_step, 0)

    t_row = thr[...]
    t_sel = jnp.maximum(t_row, KEY_NEG_INF + 1)

    def bias_chunk(c, _):
        for half in range(kc // LANES):
            hs = slice(half * LANES, (half + 1) * LANES)
            bias[c, :, hs] = jnp.where(keys[c, :, hs] >= t_sel, 0.0, NEG_BIG)
        return 0

    lax.fori_loop(0, n_chunks, bias_chunk, 0)

    tied = jnp.where((cnt_thr[...] > kf) & (t_row > KEY_NEG_INF), 1.0, 0.0)

    @pl.when(jnp.max(tied) > 0.0)
    def _():
        def count_gt(c, acc):
            for half in range(kc // LANES):
                hs = slice(half * LANES, (half + 1) * LANES)
                acc = acc + jnp.where(keys[c, :, hs] > t_row, 1.0, 0.0)
            return acc
        acc = lax.fori_loop(0, n_chunks, count_gt, jnp.zeros((tq, LANES), f32))
        need = kf - jnp.broadcast_to(jnp.sum(acc, axis=1, keepdims=True), (tq, LANES))
        r_i = lax.broadcasted_iota(jnp.int32, (LANES, LANES), 0)
        c_i = lax.broadcasted_iota(jnp.int32, (LANES, LANES), 1)
        before = jnp.where(r_i < c_i, 1.0, 0.0).astype(bf16)
        ones = jnp.ones((LANES, LANES), bf16)
        real = t_row > KEY_NEG_INF

        def tie_chunk(c, seen):
            for half in range(kc // LANES):
                hs = slice(half * LANES, (half + 1) * LANES)
                kx = keys[c, :, hs]
                eq = kx == t_row
                eqb = jnp.where(eq, 1.0, 0.0).astype(bf16)
                rank = seen + _dot(eqb, before)
                sel = (kx > t_row) | (eq & (rank < need) & real)
                bias[c, :, hs] = jnp.where(sel, 0.0, NEG_BIG)
                seen = seen + _dot(eqb, ones)
            return seen

        lax.fori_loop(0, n_chunks, tie_chunk, jnp.zeros((tq, LANES), f32))

    m_ref[...] = jnp.full(m_ref.shape, -jnp.inf, f32)
    l_ref[...] = jnp.zeros(l_ref.shape, f32)
    acc_ref[...] = jnp.zeros(acc_ref.shape, f32)

    def attend_chunk(c, _):
        start = pl.multiple_of(c * kc, kc)
        logit[...] = _dot_nt(lhs_a[...], kk_ref[pl.ds(start, kc), :])
        b = bias[c]
        v_e = vze[pl.ds(start, kc), :]
        v_o = vzo[pl.ds(start, kc), :]
        for p in range(N_PAIRS):
            alphas = []
            pv = None
            for e in range(PAIR):
                h = PAIR * p + e
                s = logit[h * tq:(h + 1) * tq, :] + b
                m_old = m_ref[h]
                m_new = jnp.maximum(m_old, jnp.max(s, axis=1, keepdims=True))
                alpha = jnp.exp(m_old - m_new)
                pm = jnp.exp(s - m_new)
                l_ref[h] = alpha * l_ref[h] + jnp.sum(pm, axis=1, keepdims=True)
                m_ref[h] = m_new
                part = _dot(pm.astype(bf16), v_e if e == 0 else v_o)
                pv = part if pv is None else pv + part
                alphas.append(alpha)
            acc_ref[p] = jnp.where(even, alphas[0], alphas[1]) * acc_ref[p] + pv
        return 0

    lax.fori_loop(0, n_chunks, attend_chunk, 0)

    for p in range(N_PAIRS):
        inv = jnp.where(even, 1.0 / l_ref[PAIR * p], 1.0 / l_ref[PAIR * p + 1])
        out_ref[:, p * LANES:(p + 1) * LANES] = (acc_ref[p] * inv).astype(out_ref.dtype)


def _dsa(aq, iq, iw, kk, vv, ik, batch, seq):
    t = aq.shape[0]
    tq = min(TQ, seq)
    kc = min(KC, tq)
    nq = seq // tq
    k_sel = min(TOPK_MAX, seq // 4)
    heads = DSA_HEADS
    qrow = lambda b, j: (b * nq + j, 0)
    brow = lambda b, j: (b, 0)
    return pl.pallas_call(
        functools.partial(_dsa_kernel, k_sel),
        grid=(batch, nq),
        in_specs=[pl.BlockSpec((tq, GROUP), qrow), pl.BlockSpec((tq, GROUP), qrow),
                  pl.BlockSpec((tq, LANES), qrow),
                  pl.BlockSpec((seq, LANES), brow), pl.BlockSpec((seq, LANES), brow),
                  pl.BlockSpec((seq, LANES), brow)],
        out_specs=pl.BlockSpec((tq, GROUP), qrow),
        out_shape=jax.ShapeDtypeStruct((t, GROUP), bf16),
        scratch_shapes=[
            pltpu.VMEM((heads * tq, LANES), bf16),
            pltpu.VMEM((heads * tq, LANES), bf16),
            pltpu.VMEM((IDX_HEADS, tq, LANES), f32),
            pltpu.VMEM((seq // kc, tq, kc), jnp.int32),
            pltpu.VMEM((seq // kc, tq, kc), f32),
            pltpu.VMEM((heads * tq, kc), f32),
            pltpu.VMEM((tq, LANES), jnp.int32),
            pltpu.VMEM((tq, LANES), f32),
            pltpu.VMEM((heads, tq, 1), f32),
            pltpu.VMEM((heads, tq, 1), f32),
            pltpu.VMEM((N_PAIRS, tq, LANES), f32),
            pltpu.VMEM((seq, LANES), bf16),
            pltpu.VMEM((seq, LANES), bf16),
        ],
        compiler_params=pltpu.CompilerParams(dimension_semantics=("arbitrary", "arbitrary"),
                                             vmem_limit_bytes=VMEM_LIMIT),
        name="dsa",
    )(aq, iq, iw, kk, vv, ik)


def _outproj_kernel(ret_ref, att_ref, x_ref, w_ref, gpost_ref, gpre_ref, h_ref, hn_ref):
    mixed = _dot(ret_ref[...], w_ref[0:GROUP, :]) + _dot(att_ref[...], w_ref[GROUP:2 * GROUP, :])
    h = x_ref[...] + _rms(mixed, gpost_ref[...])
    h_ref[...] = h
    hn_ref[...] = _rms(h, gpre_ref[...]).astype(hn_ref.dtype)


def _outproj(ret, att, x2, w, g_post, g_pre, seq):
    t, d = x2.shape
    tm = min(TM_PROJ, seq)
    row = lambda i: (i, 0)
    fixed = lambda i: (0, 0)
    return pl.pallas_call(
        _outproj_kernel,
        grid=(t // tm,),
        in_specs=[pl.BlockSpec((tm, GROUP), row), pl.BlockSpec((tm, GROUP), row),
                  pl.BlockSpec((tm, d), row), pl.BlockSpec((2 * GROUP, d), fixed),
                  pl.BlockSpec((1, d), fixed), pl.BlockSpec((1, d), fixed)],
        out_specs=[pl.BlockSpec((tm, d), row), pl.BlockSpec((tm, d), row)],
        out_shape=[jax.ShapeDtypeStruct((t, d), f32), jax.ShapeDtypeStruct((t, d), bf16)],
        compiler_params=pltpu.CompilerParams(dimension_semantics=("arbitrary",),
                                             vmem_limit_bytes=VMEM_LIMIT),
        name="out_proj",
    )(ret, att, x2, w, g_post, g_pre)


def _ffn_kernel(hn_ref, h_ref, wg_ref, wu_ref, cwg_ref, cwu_ref, cbg_ref, cbu_ref, wd_ref, gpost_ref,
                out_ref, acc_ref, carry_ref):
    i = pl.program_id(1)
    f = pl.program_id(2)
    tm = hn_ref.shape[0]
    row = lax.broadcasted_iota(jnp.int32, (tm, wg_ref.shape[1]), 0)

    @pl.when(f == 0)
    def _():
        acc_ref[...] = jnp.zeros_like(acc_ref)

    @pl.when(i == 0)
    def _():
        carry_ref[f] = jnp.zeros(carry_ref.shape[1:], f32)

    hn = hn_ref[...]

    def conv(up, w_ref, b_ref, tail):
        m1 = jnp.where(row == 0, tail[7:8, :], pltpu.roll(up, 1, 0))
        m2 = jnp.where(row == 0, tail[6:7, :], jnp.where(row == 1, tail[7:8, :], pltpu.roll(up, 2, 0)))
        return b_ref[...] + m2 * w_ref[0:1, :] + m1 * w_ref[1:2, :] + up * w_ref[2:3, :]

    up_g = _dot(hn, wg_ref[...])
    up_u = _dot(hn, wu_ref[...])
    g = conv(up_g, cwg_ref, cbg_ref, carry_ref[f, 0])
    u = conv(up_u, cwu_ref, cbu_ref, carry_ref[f, 1])
    carry_ref[f, 0] = up_g[tm - 8:tm, :]
    carry_ref[f, 1] = up_u[tm - 8:tm, :]
    acc_ref[...] += _dot((_silu(g) * u).astype(bf16), wd_ref[...])

    @pl.when(f == pl.num_programs(2) - 1)
    def _():
        out_ref[...] = h_ref[...] + _rms(acc_ref[...], gpost_ref[...])


def _ffn(hn, h, w_up, conv_w, conv_b, w_down, g_post, batch, seq):
    t, d = h.shape
    tm = min(TM_FFN, seq)
    n_seq = seq // tm
    n_f = D_FF // TF
    row = lambda b, i, f: (b * n_seq + i, 0)
    return pl.pallas_call(
        _ffn_kernel,
        grid=(batch, n_seq, n_f),
        in_specs=[pl.BlockSpec((tm, d), row), pl.BlockSpec((tm, d), row),
                  pl.BlockSpec((d, TF), lambda b, i, f: (0, f)),
                  pl.BlockSpec((d, TF), lambda b, i, f: (0, n_f + f)),
                  pl.BlockSpec((CONV_WIDTH, TF), lambda b, i, f: (0, f)),
                  pl.BlockSpec((CONV_WIDTH, TF), lambda b, i, f: (0, n_f + f)),
                  pl.BlockSpec((1, TF), lambda b, i, f: (0, f)),
                  pl.BlockSpec((1, TF), lambda b, i, f: (0, n_f + f)),
                  pl.BlockSpec((TF, d), lambda b, i, f: (f, 0)),
                  pl.BlockSpec((1, d), lambda b, i, f: (0, 0))],
        out_specs=pl.BlockSpec((tm, d), row),
        out_shape=jax.ShapeDtypeStruct((t, d), f32),
        scratch_shapes=[pltpu.VMEM((tm, d), f32), pltpu.VMEM((n_f, 2, 8, TF), f32)],
        compiler_params=pltpu.CompilerParams(dimension_semantics=("arbitrary", "arbitrary", "arbitrary"),
                                             vmem_limit_bytes=VMEM_LIMIT),
        name="ffn",
    )(hn, h, w_up, w_up, conv_w, conv_w, conv_b, conv_b, w_down, g_post)


def _relayout_w_in(w):
    d = w.shape[0]
    sizes = (GROUP, GROUP, GROUP, GROUP, GROUP, HEAD_DIM, HEAD_DIM, IDX_HEADS * IDX_DIM, IDX_DIM, IDX_HEADS)
    parts, off = [], 0
    for n in sizes:
        parts.append(w[:, off:off + n])
        off += n
    rq, rk, rv, rg, aq, ak, av, iq, ik, iw = parts
    pad = jnp.zeros((d, LANES - IDX_HEADS), w.dtype)
    return jnp.concatenate([rq, rk, aq, iq, rv, rg, ak, ak, av, av, ik, ik, iw, pad], axis=1).astype(bf16)


def _rope_tables(seq):
    half = HEAD_DIM // 2
    inv = ROPE_THETA ** (-jnp.arange(0, HEAD_DIM, 2, dtype=f32) / HEAD_DIM)
    ang = jnp.arange(seq, dtype=f32)[:, None] * inv[None, :]
    cos, sin = jnp.cos(ang), jnp.sin(ang)
    cos_t = jnp.tile(cos, (1, LANES // half))
    sin_t = jnp.tile(jnp.concatenate([-sin, sin], axis=1), (1, PAIR))
    return cos_t, sin_t


def kernel(x, mix_norm_pre, mix_norm_post, w_in, w_out, ffn_norm_pre, ffn_norm_post,
           w_up, conv_w, conv_b, w_down):
    batch, seq, d = x.shape
    depth = w_in.shape[0]
    cos_t, sin_t = _rope_tables(seq)
    h = x.reshape(batch * seq, d)
    for l in range(depth):
        rq, rk, aq, iq, rv, rg, kk, vv, ik, iw = _inproj(
            h, mix_norm_pre[l][None, :], _relayout_w_in(w_in[l]), cos_t, sin_t, seq)
        ret = _retention(rq, rk, rv, rg, batch, seq)
        att = _dsa(aq, iq, iw, kk, vv, ik, batch, seq)
        h, hn = _outproj(ret, att, h, w_out[l].astype(bf16), mix_norm_post[l][None, :],
                         ffn_norm_pre[l][None, :], seq)
        h = _ffn(hn, h, w_up[l].astype(bf16), conv_w[l], conv_b[l][None, :], w_down[l].astype(bf16),
                 ffn_norm_post[l][None, :], batch, seq)
    return h.reshape(batch, seq, d)
```

```python
import functools

import jax
import jax.numpy as jnp
from jax import lax
from jax.experimental import pallas as pl
from jax.experimental.pallas import tpu as pltpu

HEAD_DIM = 64
RET_HEADS = 8
DSA_HEADS = 8
IDX_HEADS = 8
IDX_DIM = 64
TOPK_MAX = 256
D_FF = 2816
CONV_WIDTH = 3
ROPE_THETA = 10000.0
EPS = 1e-6

LANES = 128
SUBLANES = 8
PAIR = LANES // HEAD_DIM
N_PAIRS = RET_HEADS // PAIR
GROUP = RET_HEADS * HEAD_DIM

_OFF_RQ, _OFF_RK, _OFF_AQ, _OFF_IQ, _OFF_RV, _OFF_RG = (i * GROUP for i in range(6))
_OFF_KK = 6 * GROUP
_OFF_VV = _OFF_KK + LANES
_OFF_IK = _OFF_VV + LANES
_OFF_IW = _OFF_IK + LANES
_W_TOTAL = _OFF_IW + LANES

TM_PROJ = 512
RET_CHUNK = 256
TQ = 256
KC = 256
TM_FFN = 1024
TF = 256
VMEM_LIMIT = 48 * 1024 * 1024

NEG_BIG = -1e30
INT_MIN = -2 ** 31
KEY_NEG_INF = -2139095041
N_COUNT_ACC = 4

f32 = jnp.float32
bf16 = jnp.bfloat16


def _dot(a, b):
    return jnp.dot(a, b, preferred_element_type=f32)


def _dot_nt(a, b):
    return lax.dot_general(a, b, (((1,), (1,)), ((), ())), preferred_element_type=f32)


def _dot_tn(a, b):
    return lax.dot_general(a, b, (((0,), (0,)), ((), ())), preferred_element_type=f32)


def _rms(x, g):
    return x * lax.rsqrt(jnp.mean(x * x, axis=-1, keepdims=True) + EPS) * g


def _silu(x):
    return x * (1.0 / (1.0 + jnp.exp(-x)))


def _inproj_kernel(x_ref, g_ref, w_ref, ch_ref, sh_ref,
                   rq_ref, rk_ref, aq_ref, iq_ref, rv_ref, rg_ref, kk_ref, vv_ref, ik_ref, iw_ref):
    tm = x_ref.shape[0]
    hn = _rms(x_ref[...], g_ref[...]).astype(bf16)
    cos = ch_ref[...]
    sin = sh_ref[...]
    lane = lax.broadcasted_iota(jnp.int32, (tm, LANES), 1)
    first_half = (lane % HEAD_DIM) < (HEAD_DIM // 2)

    def proj(off, width):
        return _dot(hn, w_ref[:, off:off + width])

    def rope(p):
        swapped = jnp.where(first_half, pltpu.roll(p, LANES - HEAD_DIM // 2, 1),
                            pltpu.roll(p, HEAD_DIM // 2, 1))
        return p * cos + swapped * sin

    def roped_group(off, out_ref, scale):
        p = proj(off, GROUP)
        for b in range(GROUP // LANES):
            r = rope(p[:, b * LANES:(b + 1) * LANES])
            if scale != 1.0:
                r = r * scale
            out_ref[:, b * LANES:(b + 1) * LANES] = r.astype(out_ref.dtype)

    roped_group(_OFF_RQ, rq_ref, 1.0)
    roped_group(_OFF_RK, rk_ref, HEAD_DIM ** -0.5)
    roped_group(_OFF_AQ, aq_ref, HEAD_DIM ** -0.5)
    roped_group(_OFF_IQ, iq_ref, 1.0)
    rv_ref[...] = proj(_OFF_RV, GROUP).astype(bf16)
    rg_ref[...] = proj(_OFF_RG, GROUP)
    small = proj(_OFF_KK, 4 * LANES)
    kk_ref[...] = rope(small[:, 0:LANES]).astype(bf16)
    vv_ref[...] = small[:, LANES:2 * LANES].astype(bf16)
    ik_ref[...] = rope(small[:, 2 * LANES:3 * LANES]).astype(bf16)
    iw_ref[...] = small[:, 3 * LANES:4 * LANES] * (IDX_HEADS ** -0.5 * IDX_DIM ** -0.5)


def _inproj(x2, g, w, cos_t, sin_t, seq):
    t, d = x2.shape
    tm = min(TM_PROJ, seq)
    n_seq = seq // tm
    row = lambda i: (i, 0)
    fixed = lambda i: (0, 0)
    pos = lambda i: (i % n_seq, 0)
    wide = lambda dt: jax.ShapeDtypeStruct((t, GROUP), dt)
    narrow = lambda dt: jax.ShapeDtypeStruct((t, LANES), dt)
    return pl.pallas_call(
        _inproj_kernel,
        grid=(t // tm,),
        in_specs=[pl.BlockSpec((tm, d), row), pl.BlockSpec((1, d), fixed),
                  pl.BlockSpec((d, _W_TOTAL), fixed),
                  pl.BlockSpec((tm, LANES), pos), pl.BlockSpec((tm, LANES), pos)],
        out_specs=[pl.BlockSpec((tm, GROUP), row)] * 6 + [pl.BlockSpec((tm, LANES), row)] * 4,
        out_shape=[wide(bf16), wide(bf16), wide(bf16), wide(bf16), wide(bf16), wide(f32),
                   narrow(bf16), narrow(bf16), narrow(bf16), narrow(f32)],
        compiler_params=pltpu.CompilerParams(dimension_semantics=("arbitrary",),
                                             vmem_limit_bytes=VMEM_LIMIT),
        name="in_proj",
    )(x2, g, w, cos_t, sin_t)


def _retention_kernel(rq_ref, rk_ref, rv_ref, rg_ref, decay_ref, zeta_ref, xi_ref, cd_ref,
                      out_ref, state_ref):
    c = rq_ref.shape[0]

    @pl.when(pl.program_id(1) == 0)
    def _():
        state_ref[...] = jnp.zeros_like(state_ref)

    lane = lax.broadcasted_iota(jnp.int32, (c, LANES), 1)
    even = lane < HEAD_DIM
    r_i = lax.broadcasted_iota(jnp.int32, (LANES, LANES), 0)
    c_i = lax.broadcasted_iota(jnp.int32, (LANES, LANES), 1)
    same_head = (r_i < HEAD_DIM) == (c_i < HEAD_DIM)
    ones_bd = jnp.where(same_head, 1.0, 0.0).astype(bf16)

    for p in range(N_PAIRS):
        sl = slice(p * LANES, (p + 1) * LANES)
        q2 = rq_ref[:, sl]
        k2 = rk_ref[:, sl]
        v2 = rv_ref[:, sl]
        zero = jnp.zeros_like(q2)
        s_e = _dot_nt(jnp.where(even, q2, zero), k2) * decay_ref[2 * p]
        s_o = _dot_nt(jnp.where(even, zero, q2), k2) * decay_ref[2 * p + 1]
        inner = jnp.where(even, _dot(s_e.astype(bf16), v2), _dot(s_o.astype(bf16), v2))
        r_prev = state_ref[p]
        cross = _dot(q2, r_prev.astype(bf16)) * xi_ref[p]
        o = inner + cross
        kz = (k2.astype(f32) * zeta_ref[p]).astype(bf16)
        s_new = jnp.where(same_head, _dot_tn(kz, v2), 0.0)
        state_ref[p] = cd_ref[p] * r_prev + s_new
        sq = o * o
        hi = sq.astype(bf16)
        lo = (sq - hi.astype(f32)).astype(bf16)
        ms = (_dot(hi, ones_bd) + _dot(lo, ones_bd)) * (1.0 / HEAD_DIM)
        o = o * lax.rsqrt(ms + EPS)
        out_ref[:, sl] = (o * _silu(rg_ref[:, sl])).astype(out_ref.dtype)


def _retention_tables(c):
    h = jnp.arange(RET_HEADS, dtype=f32)
    log_g = jnp.log(1.0 - 2.0 ** (-5.0 - h))
    pos = jnp.arange(c, dtype=f32)
    diff = pos[:, None] - pos[None, :]
    decay = jnp.where(diff >= 0, jnp.exp(log_g[:, None, None] * jnp.maximum(diff, 0.0)), 0.0)
    zeta = jnp.exp(log_g[:, None] * (c - 1.0 - pos)[None, :])
    xi = jnp.exp(log_g[:, None] * (pos + 1.0)[None, :])
    cd = jnp.exp(log_g * c)

    def pair_lanes(a):
        a = a.reshape(N_PAIRS, PAIR, c)
        return jnp.repeat(jnp.transpose(a, (0, 2, 1)), HEAD_DIM, axis=2)

    cd2 = jnp.repeat(cd.reshape(N_PAIRS, 1, PAIR), HEAD_DIM, axis=2)
    return decay, pair_lanes(zeta), pair_lanes(xi), cd2


def _retention(rq, rk, rv, rg, batch, seq):
    t = rq.shape[0]
    c = min(RET_CHUNK, seq)
    n = seq // c
    decay, zeta, xi, cd = _retention_tables(c)
    row = lambda b, i: (b * n + i, 0)
    fix3 = lambda b, i: (0, 0, 0)
    return pl.pallas_call(
        _retention_kernel,
        grid=(batch, n),
        in_specs=[pl.BlockSpec((c, GROUP), row)] * 4 + [
            pl.BlockSpec((RET_HEADS, c, c), fix3), pl.BlockSpec((N_PAIRS, c, LANES), fix3),
            pl.BlockSpec((N_PAIRS, c, LANES), fix3), pl.BlockSpec((N_PAIRS, 1, LANES), fix3)],
        out_specs=pl.BlockSpec((c, GROUP), row),
        out_shape=jax.ShapeDtypeStruct((t, GROUP), bf16),
        scratch_shapes=[pltpu.VMEM((N_PAIRS, LANES, LANES), f32)],
        compiler_params=pltpu.CompilerParams(dimension_semantics=("arbitrary", "arbitrary"),
                                             vmem_limit_bytes=VMEM_LIMIT),
        name="retention",
    )(rq, rk, rv, rg, decay, zeta, xi, cd)


def _key_to_float(key):
    bits = jnp.where(key >= 0, key, key ^ jnp.int32(0x7FFFFFFF))
    return lax.bitcast_convert_type(bits, f32)


def _dsa_kernel(k_sel, aq_ref, iq_ref, iw_ref, kk_ref, vv_ref, ik_ref, out_ref,
                lhs_a, lhs_i, xs, bias, logit, m_ref, acc_ref, vaug):
    tq = aq_ref.shape[0]
    n_kc, kc, _ = xs.shape
    j = pl.program_id(1)
    n_chunks = (j * tq) // kc + tq // kc
    kf = float(k_sel)

    @pl.when(j == 0)
    def _():
        row = lax.broadcasted_iota(jnp.int32, (LANES, kc), 0)
        for c in range(n_kc):
            vt = vv_ref[c * kc:(c + 1) * kc, :].astype(f32).T
            vaug[c] = jnp.where(row < HEAD_DIM, vt, 1.0).astype(bf16)

    lane = lax.broadcasted_iota(jnp.int32, (tq, LANES), 1)
    even = lane < HEAD_DIM
    for p in range(N_PAIRS):
        sl = slice(p * LANES, (p + 1) * LANES)
        a2 = aq_ref[:, sl]
        i2 = iq_ref[:, sl]
        zero = jnp.zeros_like(a2)
        lhs_a[(2 * p) * tq:(2 * p + 1) * tq, :] = jnp.where(even, a2, zero)
        lhs_a[(2 * p + 1) * tq:(2 * p + 2) * tq, :] = jnp.where(even, zero, a2)
        lhs_i[(2 * p) * tq:(2 * p + 1) * tq, :] = jnp.where(even, i2, zero)
        lhs_i[(2 * p + 1) * tq:(2 * p + 2) * tq, :] = jnp.where(even, zero, i2)
    w_heads = iw_ref[...].T[0:IDX_HEADS, :]

    key_row = lax.broadcasted_iota(jnp.int32, (kc, tq), 0)
    q_pos = j * tq + lax.broadcasted_iota(jnp.int32, (kc, tq), 1)

    def score_chunk(c, _):
        start = pl.multiple_of(c * kc, kc)
        rel = _dot_nt(ik_ref[pl.ds(start, kc), :], lhs_i[...])
        isc = jnp.zeros((kc, tq), f32)
        for h in range(IDX_HEADS):
            isc = isc + w_heads[h:h + 1, :] * jnp.maximum(rel[:, h * tq:(h + 1) * tq], 0.0)
        xs[c] = jnp.where(c * kc + key_row <= q_pos, isc, -jnp.inf)
        return 0

    lax.fori_loop(0, n_chunks, score_chunk, 0)

    def count_ge(cand_f):
        def body(c, accs):
            x = xs[c]
            accs = list(accs)
            for r in range(kc // SUBLANES):
                hit = jnp.where(x[r * SUBLANES:(r + 1) * SUBLANES, :] >= cand_f, 1.0, 0.0)
                accs[r % N_COUNT_ACC] = accs[r % N_COUNT_ACC] + hit
            return tuple(accs)
        zero = jnp.zeros((SUBLANES, tq), f32)
        accs = lax.fori_loop(0, n_chunks, body, (zero,) * N_COUNT_ACC)
        total = functools.reduce(lambda a, b: a + b, accs)
        return jnp.broadcast_to(jnp.sum(total, axis=0, keepdims=True), (SUBLANES, tq))

    def bit_step(i, carry):
        t_key, t_cnt = carry
        cand = t_key + lax.shift_left(jnp.int32(1), jnp.int32(31) - i)
        cnt = count_ge(_key_to_float(cand))
        ok = cnt >= kf
        return jnp.where(ok, cand, t_key), jnp.where(ok, cnt, t_cnt)

    t_key, t_cnt = lax.fori_loop(
        0, 32, bit_step,
        (jnp.full((SUBLANES, tq), INT_MIN, jnp.int32),
         jnp.zeros((SUBLANES, tq), f32) + (n_chunks * kc).astype(f32)))

    t_sel = _key_to_float(jnp.maximum(t_key, KEY_NEG_INF + 1))[0:1, :]

    def bias_chunk(c, _):
        bias[c] = jnp.where(xs[c] >= t_sel, 0.0, NEG_BIG)
        return 0

    lax.fori_loop(0, n_chunks, bias_chunk, 0)

    real = t_key > KEY_NEG_INF
    tied = jnp.where((t_cnt > kf) & real, 1.0, 0.0)

    @pl.when(jnp.max(tied) > 0.0)
    def _():
        t_val = _key_to_float(t_key)[0:1, :]
        real_row = real[0:1, :]

        def count_gt(c, acc):
            return acc + jnp.sum(jnp.where(xs[c] > t_val, 1.0, 0.0), axis=0, keepdims=True)
        need = kf - lax.fori_loop(0, n_chunks, count_gt, jnp.zeros((1, tq), f32))
        r_i = lax.broadcasted_iota(jnp.int32, (kc, kc), 0)
        c_i = lax.broadcasted_iota(jnp.int32, (kc, kc), 1)
        earlier = jnp.where(c_i < r_i, 1.0, 0.0).astype(bf16)

        def tie_chunk(c, seen):
            x = xs[c]
            eq = x == t_val
            eq_f = jnp.where(eq, 1.0, 0.0)
            rank = seen + _dot(earlier, eq_f.astype(bf16))
            sel = (real_row & ((x > t_val) | (eq & (rank < need)))) | ((~real_row) & (x >= t_sel))
            bias[c] = jnp.where(sel, 0.0, NEG_BIG)
            return seen + jnp.sum(eq_f, axis=0, keepdims=True)

        lax.fori_loop(0, n_chunks, tie_chunk, jnp.zeros((1, tq), f32))

    m_ref[...] = jnp.full(m_ref.shape, -jnp.inf, f32)
    acc_ref[...] = jnp.zeros(acc_ref.shape, f32)

    def attend_chunk(c, _):
        start = pl.multiple_of(c * kc, kc)
        logit[...] = _dot_nt(kk_ref[pl.ds(start, kc), :], lhs_a[...])
        b = bias[c]
        va = vaug[c]
        for h in range(DSA_HEADS):
            s = logit[:, h * tq:(h + 1) * tq] + b
            m_old = m_ref[h]
            m_new = jnp.maximum(m_old, jnp.max(s, axis=0, keepdims=True))
            alpha = jnp.exp(m_old - m_new)
            p = jnp.exp(s - m_new).astype(bf16)
            acc_ref[h] = alpha * acc_ref[h] + _dot(va, p)
            m_ref[h] = m_new
        return 0

    lax.fori_loop(0, n_chunks, attend_chunk, 0)

    for p in range(N_PAIRS):
        halves = []
        for e in range(PAIR):
            a = acc_ref[PAIR * p + e]
            halves.append(a[0:HEAD_DIM, :] * (1.0 / a[HEAD_DIM:HEAD_DIM + 1, :]))
        out_ref[:, p * LANES:(p + 1) * LANES] = jnp.concatenate(halves, axis=0).T.astype(out_ref.dtype)


def _dsa(aq, iq, iw, kk, vv, ik, batch, seq):
    t = aq.shape[0]
    tq = min(TQ, seq)
    kc = min(KC, tq)
    nq = seq // tq
    k_sel = min(TOPK_MAX, seq // 4)
    heads = DSA_HEADS
    qrow = lambda b, j: (b * nq + j, 0)
    brow = lambda b, j: (b, 0)
    return pl.pallas_call(
        functools.partial(_dsa_kernel, k_sel),
        grid=(batch, nq),
        in_specs=[pl.BlockSpec((tq, GROUP), qrow), pl.BlockSpec((tq, GROUP), qrow),
                  pl.BlockSpec((tq, LANES), qrow),
                  pl.BlockSpec((seq, LANES), brow), pl.BlockSpec((seq, LANES), brow),
                  pl.BlockSpec((seq, LANES), brow)],
        out_specs=pl.BlockSpec((tq, GROUP), qrow),
        out_shape=jax.ShapeDtypeStruct((t, GROUP), bf16),
        scratch_shapes=[
            pltpu.VMEM((heads * tq, LANES), bf16),
            pltpu.VMEM((heads * tq, LANES), bf16),
            pltpu.VMEM((seq // kc, kc, tq), f32),
            pltpu.VMEM((seq // kc, kc, tq), f32),
            pltpu.VMEM((kc, heads * tq), f32),
            pltpu.VMEM((heads, 1, tq), f32),
            pltpu.VMEM((heads, LANES, tq), f32),
            pltpu.VMEM((seq // kc, LANES, kc), bf16),
        ],
        compiler_params=pltpu.CompilerParams(dimension_semantics=("arbitrary", "arbitrary"),
                                             vmem_limit_bytes=VMEM_LIMIT),
        name="dsa",
    )(aq, iq, iw, kk, vv, ik)


def _outproj_kernel(ret_ref, att_ref, x_ref, w_ref, gpost_ref, gpre_ref, h_ref, hn_ref):
    mixed = _dot(ret_ref[...], w_ref[0:GROUP, :]) + _dot(att_ref[...], w_ref[GROUP:2 * GROUP, :])
    h = x_ref[...] + _rms(mixed, gpost_ref[...])
    h_ref[...] = h
    hn_ref[...] = _rms(h, gpre_ref[...]).astype(hn_ref.dtype)


def _outproj(ret, att, x2, w, g_post, g_pre, seq):
    t, d = x2.shape
    tm = min(TM_PROJ, seq)
    row = lambda i: (i, 0)
    fixed = lambda i: (0, 0)
    return pl.pallas_call(
        _outproj_kernel,
        grid=(t // tm,),
        in_specs=[pl.BlockSpec((tm, GROUP), row), pl.BlockSpec((tm, GROUP), row),
                  pl.BlockSpec((tm, d), row), pl.BlockSpec((2 * GROUP, d), fixed),
                  pl.BlockSpec((1, d), fixed), pl.BlockSpec((1, d), fixed)],
        out_specs=[pl.BlockSpec((tm, d), row), pl.BlockSpec((tm, d), row)],
        out_shape=[jax.ShapeDtypeStruct((t, d), f32), jax.ShapeDtypeStruct((t, d), bf16)],
        compiler_params=pltpu.CompilerParams(dimension_semantics=("arbitrary",),
                                             vmem_limit_bytes=VMEM_LIMIT),
        name="out_proj",
    )(ret, att, x2, w, g_post, g_pre)


def _ffn_kernel(hn_ref, h_ref, wg_ref, wu_ref, cwg_ref, cwu_ref, cbg_ref, cbu_ref, wd_ref, gpost_ref,
                out_ref, acc_ref, carry_ref):
    i = pl.program_id(1)
    f = pl.program_id(2)
    tm = hn_ref.shape[0]
    row = lax.broadcasted_iota(jnp.int32, (tm, wg_ref.shape[1]), 0)

    @pl.when(f == 0)
    def _():
        acc_ref[...] = jnp.zeros_like(acc_ref)

    @pl.when(i == 0)
    def _():
        carry_ref[f] = jnp.zeros(carry_ref.shape[1:], f32)

    hn = hn_ref[...]

    def conv(up, w_ref, b_ref, tail):
        m1 = jnp.where(row == 0, tail[7:8, :], pltpu.roll(up, 1, 0))
        m2 = jnp.where(row == 0, tail[6:7, :], jnp.where(row == 1, tail[7:8, :], pltpu.roll(up, 2, 0)))
        return b_ref[...] + m2 * w_ref[0:1, :] + m1 * w_ref[1:2, :] + up * w_ref[2:3, :]

    up_g = _dot(hn, wg_ref[...])
    up_u = _dot(hn, wu_ref[...])
    g = conv(up_g, cwg_ref, cbg_ref, carry_ref[f, 0])
    u = conv(up_u, cwu_ref, cbu_ref, carry_ref[f, 1])
    carry_ref[f, 0] = up_g[tm - 8:tm, :]
    carry_ref[f, 1] = up_u[tm - 8:tm, :]
    acc_ref[...] += _dot((_silu(g) * u).astype(bf16), wd_ref[...])

    @pl.when(f == pl.num_programs(2) - 1)
    def _():
        out_ref[...] = h_ref[...] + _rms(acc_ref[...], gpost_ref[...])


def _ffn(hn, h, w_up, conv_w, conv_b, w_down, g_post, batch, seq):
    t, d = h.shape
    tm = min(TM_FFN, seq)
    n_seq = seq // tm
    n_f = D_FF // TF
    row = lambda b, i, f: (b * n_seq + i, 0)
    return pl.pallas_call(
        _ffn_kernel,
        grid=(batch, n_seq, n_f),
        in_specs=[pl.BlockSpec((tm, d), row), pl.BlockSpec((tm, d), row),
                  pl.BlockSpec((d, TF), lambda b, i, f: (0, f)),
                  pl.BlockSpec((d, TF), lambda b, i, f: (0, n_f + f)),
                  pl.BlockSpec((CONV_WIDTH, TF), lambda b, i, f: (0, f)),
                  pl.BlockSpec((CONV_WIDTH, TF), lambda b, i, f: (0, n_f + f)),
                  pl.BlockSpec((1, TF), lambda b, i, f: (0, f)),
                  pl.BlockSpec((1, TF), lambda b, i, f: (0, n_f + f)),
                  pl.BlockSpec((TF, d), lambda b, i, f: (f, 0)),
                  pl.BlockSpec((1, d), lambda b, i, f: (0, 0))],
        out_specs=pl.BlockSpec((tm, d), row),
        out_shape=jax.ShapeDtypeStruct((t, d), f32),
        scratch_shapes=[pltpu.VMEM((tm, d), f32), pltpu.VMEM((n_f, 2, 8, TF), f32)],
        compiler_params=pltpu.CompilerParams(dimension_semantics=("arbitrary", "arbitrary", "arbitrary"),
                                             vmem_limit_bytes=VMEM_LIMIT),
        name="ffn",
    )(hn, h, w_up, w_up, conv_w, conv_w, conv_b, conv_b, w_down, g_post)


def _relayout_w_in(w):
    d = w.shape[0]
    sizes = (GROUP, GROUP, GROUP, GROUP, GROUP, HEAD_DIM, HEAD_DIM, IDX_HEADS * IDX_DIM, IDX_DIM, IDX_HEADS)
    parts, off = [], 0
    for n in sizes:
        parts.append(w[:, off:off + n])
        off += n
    rq, rk, rv, rg, aq, ak, av, iq, ik, iw = parts
    pad = jnp.zeros((d, LANES - IDX_HEADS), w.dtype)
    return jnp.concatenate([rq, rk, aq, iq, rv, rg, ak, ak, av, av, ik, ik, iw, pad], axis=1).astype(bf16)


def _rope_tables(seq):
    half = HEAD_DIM // 2
    inv = ROPE_THETA ** (-jnp.arange(0, HEAD_DIM, 2, dtype=f32) / HEAD_DIM)
    ang = jnp.arange(seq, dtype=f32)[:, None] * inv[None, :]
    cos, sin = jnp.cos(ang), jnp.sin(ang)
    cos_t = jnp.tile(cos, (1, LANES // half))
    sin_t = jnp.tile(jnp.concatenate([-sin, sin], axis=1), (1, PAIR))
    return cos_t, sin_t


def kernel(x, mix_norm_pre, mix_norm_post, w_in, w_out, ffn_norm_pre, ffn_norm_post,
           w_up, conv_w, conv_b, w_down):
    batch, seq, d = x.shape
    depth = w_in.shape[0]
    cos_t, sin_t = _rope_tables(seq)
    h = x.reshape(batch * seq, d)
    for l in range(depth):
        rq, rk, aq, iq, rv, rg, kk, vv, ik, iw = _inproj(
            h, mix_norm_pre[l][None, :], _relayout_w_in(w_in[l]), cos_t, sin_t, seq)
        ret = _retention(rq, rk, rv, rg, batch, seq)
        att = _dsa(aq, iq, iw, kk, vv, ik, batch, seq)
        h, hn = _outproj(ret, att, h, w_out[l].astype(bf16), mix_norm_post[l][None, :],
                         ffn_norm_pre[l][None, :], seq)
        h = _ffn(hn, h, w_up[l].astype(bf16), conv_w[l], conv_b[l][None, :], w_down[l].astype(bf16),
                 ffn_norm_post[l][None, :], batch, seq)
    return h.reshape(batch, seq, d)
```

```python
import functools

import jax
import jax.numpy as jnp
from jax import lax
from jax.experimental import pallas as pl
from jax.experimental.pallas import tpu as pltpu

HEAD_DIM = 64
RET_HEADS = 8
DSA_HEADS = 8
IDX_HEADS = 8
IDX_DIM = 64
TOPK_MAX = 256
D_FF = 2816
CONV_WIDTH = 3
ROPE_THETA = 10000.0
EPS = 1e-6

LANES = 128
SUBLANES = 8
PAIR = LANES // HEAD_DIM
N_PAIRS = RET_HEADS // PAIR
GROUP = RET_HEADS * HEAD_DIM

_OFF_RQ, _OFF_RK, _OFF_AQ, _OFF_IQ, _OFF_RV, _OFF_RG = (i * GROUP for i in range(6))
_OFF_KK = 6 * GROUP
_OFF_VV = _OFF_KK + LANES
_OFF_IK = _OFF_VV + LANES
_OFF_IW = _OFF_IK + LANES
_W_TOTAL = _OFF_IW + LANES

TM_PROJ = 512
RET_CHUNK = 256
TQ = 256
KC = 512
TM_FFN = 1024
TF = 256
VMEM_LIMIT = 48 * 1024 * 1024

NEG_BIG = -1e30
INT_MIN = -2 ** 31
KEY_NEG_INF = -2139095041
N_COUNT_ACC = 4
N_REFINE = 12
SUM_ROWS = 16
LOG2E = 1.4426950408889634

f32 = jnp.float32
bf16 = jnp.bfloat16


def _dot(a, b):
    return jnp.dot(a, b, preferred_element_type=f32)


def _dot_nt(a, b):
    return lax.dot_general(a, b, (((1,), (1,)), ((), ())), preferred_element_type=f32)


def _dot_tn(a, b):
    return lax.dot_general(a, b, (((0,), (0,)), ((), ())), preferred_element_type=f32)


def _rms(x, g):
    return x * lax.rsqrt(jnp.mean(x * x, axis=-1, keepdims=True) + EPS) * g


def _silu(x):
    return x * (1.0 / (1.0 + jnp.exp(-x)))


def _inproj_kernel(x_ref, g_ref, w_ref, ch_ref, sh_ref,
                   rq_ref, rk_ref, aq_ref, iq_ref, rv_ref, rg_ref, kk_ref, vv_ref, ik_ref, iw_ref):
    tm = x_ref.shape[0]
    hn = _rms(x_ref[...], g_ref[...]).astype(bf16)
    cos = ch_ref[...]
    sin = sh_ref[...]
    lane = lax.broadcasted_iota(jnp.int32, (tm, LANES), 1)
    first_half = (lane % HEAD_DIM) < (HEAD_DIM // 2)

    def proj(off, width):
        return _dot(hn, w_ref[:, off:off + width])

    def rope(p):
        swapped = jnp.where(first_half, pltpu.roll(p, LANES - HEAD_DIM // 2, 1),
                            pltpu.roll(p, HEAD_DIM // 2, 1))
        return p * cos + swapped * sin

    def roped_group(off, out_ref, scale):
        p = proj(off, GROUP)
        for b in range(GROUP // LANES):
            r = rope(p[:, b * LANES:(b + 1) * LANES])
            if scale != 1.0:
                r = r * scale
            out_ref[:, b * LANES:(b + 1) * LANES] = r.astype(out_ref.dtype)

    roped_group(_OFF_RQ, rq_ref, 1.0)
    roped_group(_OFF_RK, rk_ref, HEAD_DIM ** -0.5)
    roped_group(_OFF_AQ, aq_ref, HEAD_DIM ** -0.5 * LOG2E)
    roped_group(_OFF_IQ, iq_ref, 1.0)
    rv_ref[...] = proj(_OFF_RV, GROUP).astype(bf16)
    rg_ref[...] = proj(_OFF_RG, GROUP)
    small = proj(_OFF_KK, 4 * LANES)
    kk_ref[...] = rope(small[:, 0:LANES]).astype(bf16)
    vv_ref[...] = small[:, LANES:2 * LANES].astype(bf16)
    ik_ref[...] = rope(small[:, 2 * LANES:3 * LANES]).astype(bf16)
    iw_ref[...] = small[:, 3 * LANES:4 * LANES] * (IDX_HEADS ** -0.5 * IDX_DIM ** -0.5)


def _inproj(x2, g, w, cos_t, sin_t, seq):
    t, d = x2.shape
    tm = min(TM_PROJ, seq)
    n_seq = seq // tm
    row = lambda i: (i, 0)
    fixed = lambda i: (0, 0)
    pos = lambda i: (i % n_seq, 0)
    wide = lambda dt: jax.ShapeDtypeStruct((t, GROUP), dt)
    narrow = lambda dt: jax.ShapeDtypeStruct((t, LANES), dt)
    return pl.pallas_call(
        _inproj_kernel,
        grid=(t // tm,),
        in_specs=[pl.BlockSpec((tm, d), row), pl.BlockSpec((1, d), fixed),
                  pl.BlockSpec((d, _W_TOTAL), fixed),
                  pl.BlockSpec((tm, LANES), pos), pl.BlockSpec((tm, LANES), pos)],
        out_specs=[pl.BlockSpec((tm, GROUP), row)] * 6 + [pl.BlockSpec((tm, LANES), row)] * 4,
        out_shape=[wide(bf16), wide(bf16), wide(bf16), wide(bf16), wide(bf16), wide(f32),
                   narrow(bf16), narrow(bf16), narrow(bf16), narrow(f32)],
        compiler_params=pltpu.CompilerParams(dimension_semantics=("arbitrary",),
                                             vmem_limit_bytes=VMEM_LIMIT),
        name="in_proj",
    )(x2, g, w, cos_t, sin_t)


def _retention_kernel(rq_ref, rk_ref, rv_ref, rg_ref, decay_ref, zeta_ref, xi_ref, cd_ref,
                      out_ref, state_ref):
    c = rq_ref.shape[0]

    @pl.when(pl.program_id(1) == 0)
    def _():
        state_ref[...] = jnp.zeros_like(state_ref)

    lane = lax.broadcasted_iota(jnp.int32, (c, LANES), 1)
    even = lane < HEAD_DIM
    r_i = lax.broadcasted_iota(jnp.int32, (LANES, LANES), 0)
    c_i = lax.broadcasted_iota(jnp.int32, (LANES, LANES), 1)
    same_head = (r_i < HEAD_DIM) == (c_i < HEAD_DIM)
    ones_bd = jnp.where(same_head, 1.0, 0.0).astype(bf16)

    for p in range(N_PAIRS):
        sl = slice(p * LANES, (p + 1) * LANES)
        q2 = rq_ref[:, sl]
        k2 = rk_ref[:, sl]
        v2 = rv_ref[:, sl]
        zero = jnp.zeros_like(q2)
        s_e = _dot_nt(jnp.where(even, q2, zero), k2) * decay_ref[2 * p]
        s_o = _dot_nt(jnp.where(even, zero, q2), k2) * decay_ref[2 * p + 1]
        inner = jnp.where(even, _dot(s_e.astype(bf16), v2), _dot(s_o.astype(bf16), v2))
        r_prev = state_ref[p]
        cross = _dot(q2, r_prev.astype(bf16)) * xi_ref[p]
        o = inner + cross
        kz = (k2.astype(f32) * zeta_ref[p]).astype(bf16)
        s_new = jnp.where(same_head, _dot_tn(kz, v2), 0.0)
        state_ref[p] = cd_ref[p] * r_prev + s_new
        sq = o * o
        hi = sq.astype(bf16)
        lo = (sq - hi.astype(f32)).astype(bf16)
        ms = (_dot(hi, ones_bd) + _dot(lo, ones_bd)) * (1.0 / HEAD_DIM)
        o = o * lax.rsqrt(ms + EPS)
        out_ref[:, sl] = (o * _silu(rg_ref[:, sl])).astype(out_ref.dtype)


def _retention_tables(c):
    h = jnp.arange(RET_HEADS, dtype=f32)
    log_g = jnp.log(1.0 - 2.0 ** (-5.0 - h))
    pos = jnp.arange(c, dtype=f32)
    diff = pos[:, None] - pos[None, :]
    decay = jnp.where(diff >= 0, jnp.exp(log_g[:, None, None] * jnp.maximum(diff, 0.0)), 0.0)
    zeta = jnp.exp(log_g[:, None] * (c - 1.0 - pos)[None, :])
    xi = jnp.exp(log_g[:, None] * (pos + 1.0)[None, :])
    cd = jnp.exp(log_g * c)

    def pair_lanes(a):
        a = a.reshape(N_PAIRS, PAIR, c)
        return jnp.repeat(jnp.transpose(a, (0, 2, 1)), HEAD_DIM, axis=2)

    cd2 = jnp.repeat(cd.reshape(N_PAIRS, 1, PAIR), HEAD_DIM, axis=2)
    return decay, pair_lanes(zeta), pair_lanes(xi), cd2


def _retention(rq, rk, rv, rg, batch, seq):
    t = rq.shape[0]
    c = min(RET_CHUNK, seq)
    n = seq // c
    decay, zeta, xi, cd = _retention_tables(c)
    row = lambda b, i: (b * n + i, 0)
    fix3 = lambda b, i: (0, 0, 0)
    return pl.pallas_call(
        _retention_kernel,
        grid=(batch, n),
        in_specs=[pl.BlockSpec((c, GROUP), row)] * 4 + [
            pl.BlockSpec((RET_HEADS, c, c), fix3), pl.BlockSpec((N_PAIRS, c, LANES), fix3),
            pl.BlockSpec((N_PAIRS, c, LANES), fix3), pl.BlockSpec((N_PAIRS, 1, LANES), fix3)],
        out_specs=pl.BlockSpec((c, GROUP), row),
        out_shape=jax.ShapeDtypeStruct((t, GROUP), bf16),
        scratch_shapes=[pltpu.VMEM((N_PAIRS, LANES, LANES), f32)],
        compiler_params=pltpu.CompilerParams(dimension_semantics=("arbitrary", "arbitrary"),
                                             vmem_limit_bytes=VMEM_LIMIT),
        name="retention",
    )(rq, rk, rv, rg, decay, zeta, xi, cd)


def _key_to_float(key):
    bits = jnp.where(key >= 0, key, key ^ jnp.int32(0x7FFFFFFF))
    return lax.bitcast_convert_type(bits, f32)


def _dsa_kernel(k_sel, aq_ref, iq_ref, iw_ref, kk_ref, vv_ref, ik_ref, out_ref,
                lhs_a, lhs_i, xs, bias, prod_a, prod_b, p_ref, m_ref, alpha_ref, acc_ref, vaug):
    tq = aq_ref.shape[0]
    n_kc, kc, _ = xs.shape
    j = pl.program_id(1)
    n_chunks = ((j + 1) * tq + kc - 1) // kc
    kf = float(k_sel)

    @pl.when(j == 0)
    def _():
        row = lax.broadcasted_iota(jnp.int32, (LANES, kc), 0)
        for c in range(n_kc):
            vt = vv_ref[c * kc:(c + 1) * kc, :].astype(f32).T
            vaug[c] = jnp.where(row < HEAD_DIM, vt, 1.0)[0:HEAD_DIM + SUM_ROWS, :].astype(bf16)

    lane = lax.broadcasted_iota(jnp.int32, (tq, LANES), 1)
    even = lane < HEAD_DIM
    for p in range(N_PAIRS):
        sl = slice(p * LANES, (p + 1) * LANES)
        a2 = aq_ref[:, sl]
        i2 = iq_ref[:, sl]
        zero = jnp.zeros_like(a2)
        lhs_a[(2 * p) * tq:(2 * p + 1) * tq, :] = jnp.where(even, a2, zero)
        lhs_a[(2 * p + 1) * tq:(2 * p + 2) * tq, :] = jnp.where(even, zero, a2)
        lhs_i[(2 * p) * tq:(2 * p + 1) * tq, :] = jnp.where(even, i2, zero)
        lhs_i[(2 * p + 1) * tq:(2 * p + 2) * tq, :] = jnp.where(even, zero, i2)
    w_heads = iw_ref[...].T[0:IDX_HEADS, :]

    key_row = lax.broadcasted_iota(jnp.int32, (kc, tq), 0)
    q_pos = j * tq + lax.broadcasted_iota(jnp.int32, (kc, tq), 1)

    def pipelined(key_ref, lhs_ref, work):
        def fill(dst, c):
            start = pl.multiple_of(jnp.minimum(c, n_chunks - 1) * kc, kc)
            dst[...] = _dot_nt(key_ref[pl.ds(start, kc), :], lhs_ref[...])

        def one_chunk(c, _):
            fill(prod_a, c)
            work(c, prod_a)
            return 0

        lax.fori_loop(0, n_chunks, one_chunk, 0)

    def score_chunk(c, prod):
        isc = jnp.zeros((kc, tq), f32)
        for h in range(IDX_HEADS):
            isc = isc + w_heads[h:h + 1, :] * jnp.maximum(prod[:, h * tq:(h + 1) * tq], 0.0)
        xs[c] = jnp.where(c * kc + key_row <= q_pos, isc, -jnp.inf)

    pipelined(ik_ref, lhs_i, score_chunk)

    def count_ge(cand_f):
        def body(c, accs):
            x = xs[c]
            accs = list(accs)
            for r in range(kc // SUBLANES):
                hit = jnp.where(x[r * SUBLANES:(r + 1) * SUBLANES, :] >= cand_f, 1.0, 0.0)
                accs[r % N_COUNT_ACC] = accs[r % N_COUNT_ACC] + hit
            return tuple(accs)
        zero = jnp.zeros((SUBLANES, tq), f32)
        accs = lax.fori_loop(0, n_chunks, body, (zero,) * N_COUNT_ACC)
        total = functools.reduce(lambda a, b: a + b, accs)
        return jnp.broadcast_to(jnp.sum(total, axis=0, keepdims=True), (SUBLANES, tq))

    def bit_step(i, carry):
        t_key, t_cnt = carry
        cand = t_key + lax.shift_left(jnp.int32(1), jnp.int32(31) - i)
        cnt = count_ge(_key_to_float(cand))
        ok = cnt >= kf
        return jnp.where(ok, cand, t_key), jnp.where(ok, cnt, t_cnt)

    t_key, t_cnt = lax.fori_loop(
        0, 32, bit_step,
        (jnp.full((SUBLANES, tq), INT_MIN, jnp.int32),
         jnp.zeros((SUBLANES, tq), f32) + (n_chunks * kc).astype(f32)))

    t_sel = _key_to_float(jnp.maximum(t_key, KEY_NEG_INF + 1))[0:1, :]

    def bias_chunk(c, _):
        bias[c] = jnp.where(xs[c] >= t_sel, 0.0, NEG_BIG).astype(bias.dtype)
        return 0

    lax.fori_loop(0, n_chunks, bias_chunk, 0)

    real = t_key > KEY_NEG_INF
    tied = jnp.where((t_cnt > kf) & real, 1.0, 0.0)

    @pl.when(jnp.max(tied) > 0.0)
    def _():
        real_row = real[0:1, :]

        def count_at_least(v):
            def body(c, acc):
                return acc + jnp.sum(jnp.where(xs[c] >= v, 1.0, 0.0), axis=0, keepdims=True)
            return lax.fori_loop(0, n_chunks, body, jnp.zeros((1, tq), f32))

        def refine(_, carry):
            lo, hi = carry
            mid = lo + (hi - lo) * 0.5
            ok = count_at_least(mid) >= kf
            return jnp.where(ok, mid, lo), jnp.where(ok, hi, mid)

        lo, hi = lax.fori_loop(0, N_REFINE, refine,
                               (_key_to_float(t_key)[0:1, :], _key_to_float(t_key + 1)[0:1, :]))
        need = kf - count_at_least(hi)
        r_i = lax.broadcasted_iota(jnp.int32, (kc, kc), 0)
        c_i = lax.broadcasted_iota(jnp.int32, (kc, kc), 1)
        earlier = jnp.where(c_i < r_i, 1.0, 0.0).astype(bf16)

        def tie_chunk(c, seen):
            x = xs[c]
            above = x >= hi
            tie = (x >= lo) & (~above)
            tie_f = jnp.where(tie, 1.0, 0.0)
            rank = seen + _dot(earlier, tie_f.astype(bf16))
            sel = (real_row & (above | (tie & (rank < need)))) | ((~real_row) & (x >= t_sel))
            bias[c] = jnp.where(sel, 0.0, NEG_BIG).astype(bias.dtype)
            return seen + jnp.sum(tie_f, axis=0, keepdims=True)

        lax.fori_loop(0, n_chunks, tie_chunk, jnp.zeros((1, tq), f32))

    m_ref[...] = jnp.full(m_ref.shape, -jnp.inf, f32)
    acc_ref[...] = jnp.zeros(acc_ref.shape, f32)

    def attend_chunk(c, prod):
        b = bias[c]
        for h in range(DSA_HEADS):
            hs = slice(h * tq, (h + 1) * tq)
            s = prod[:, hs].astype(bf16) + b
            m_old = m_ref[:, hs]
            m_new = jnp.maximum(m_old, jnp.max(s, axis=0, keepdims=True).astype(f32))
            alpha_ref[:, hs] = jnp.exp2(m_old - m_new)
            p_ref[:, hs] = jnp.exp2(s - m_new.astype(bf16))
            m_ref[:, hs] = m_new
        acc_ref[...] = alpha_ref[...] * acc_ref[...] + _dot(vaug[c], p_ref[...])

    pipelined(kk_ref, lhs_a, attend_chunk)

    for p in range(N_PAIRS):
        halves = []
        for e in range(PAIR):
            hs = slice((PAIR * p + e) * tq, (PAIR * p + e + 1) * tq)
            halves.append(acc_ref[0:HEAD_DIM, hs] * (1.0 / acc_ref[HEAD_DIM:HEAD_DIM + 1, hs]))
        out_ref[:, p * LANES:(p + 1) * LANES] = jnp.concatenate(halves, axis=0).T.astype(out_ref.dtype)


def _dsa(aq, iq, iw, kk, vv, ik, batch, seq):
    t = aq.shape[0]
    tq = min(TQ, seq)
    kc = min(KC, seq)
    nq = seq // tq
    k_sel = min(TOPK_MAX, seq // 4)
    heads = DSA_HEADS
    qrow = lambda b, j: (b * nq + j, 0)
    brow = lambda b, j: (b, 0)
    return pl.pallas_call(
        functools.partial(_dsa_kernel, k_sel),
        grid=(batch, nq),
        in_specs=[pl.BlockSpec((tq, GROUP), qrow), pl.BlockSpec((tq, GROUP), qrow),
                  pl.BlockSpec((tq, LANES), qrow),
                  pl.BlockSpec((seq, LANES), brow), pl.BlockSpec((seq, LANES), brow),
                  pl.BlockSpec((seq, LANES), brow)],
        out_specs=pl.BlockSpec((tq, GROUP), qrow),
        out_shape=jax.ShapeDtypeStruct((t, GROUP), bf16),
        scratch_shapes=[
            pltpu.VMEM((heads * tq, LANES), bf16),
            pltpu.VMEM((heads * tq, LANES), bf16),
            pltpu.VMEM((seq // kc, kc, tq), f32),
            pltpu.VMEM((seq // kc, kc, tq), bf16),
            pltpu.VMEM((kc, heads * tq), f32),
            pltpu.VMEM((kc, heads * tq), f32),
            pltpu.VMEM((kc, heads * tq), bf16),
            pltpu.VMEM((1, heads * tq), f32),
            pltpu.VMEM((1, heads * tq), f32),
            pltpu.VMEM((HEAD_DIM + SUM_ROWS, heads * tq), f32),
            pltpu.VMEM((seq // kc, HEAD_DIM + SUM_ROWS, kc), bf16),
        ],
        compiler_params=pltpu.CompilerParams(dimension_semantics=("arbitrary", "arbitrary"),
                                             vmem_limit_bytes=VMEM_LIMIT),
        name="dsa",
    )(aq, iq, iw, kk, vv, ik)


def _outproj_kernel(ret_ref, att_ref, x_ref, w_ref, gpost_ref, gpre_ref, h_ref, hn_ref):
    mixed = _dot(ret_ref[...], w_ref[0:GROUP, :]) + _dot(att_ref[...], w_ref[GROUP:2 * GROUP, :])
    h = x_ref[...] + _rms(mixed, gpost_ref[...])
    h_ref[...] = h
    hn_ref[...] = _rms(h, gpre_ref[...]).astype(hn_ref.dtype)


def _outproj(ret, att, x2, w, g_post, g_pre, seq):
    t, d = x2.shape
    tm = min(TM_PROJ, seq)
    row = lambda i: (i, 0)
    fixed = lambda i: (0, 0)
    return pl.pallas_call(
        _outproj_kernel,
        grid=(t // tm,),
        in_specs=[pl.BlockSpec((tm, GROUP), row), pl.BlockSpec((tm, GROUP), row),
                  pl.BlockSpec((tm, d), row), pl.BlockSpec((2 * GROUP, d), fixed),
                  pl.BlockSpec((1, d), fixed), pl.BlockSpec((1, d), fixed)],
        out_specs=[pl.BlockSpec((tm, d), row), pl.BlockSpec((tm, d), row)],
        out_shape=[jax.ShapeDtypeStruct((t, d), f32), jax.ShapeDtypeStruct((t, d), bf16)],
        compiler_params=pltpu.CompilerParams(dimension_semantics=("arbitrary",),
                                             vmem_limit_bytes=VMEM_LIMIT),
        name="out_proj",
    )(ret, att, x2, w, g_post, g_pre)


def _ffn_kernel(hn_ref, h_ref, wg_ref, wu_ref, cwg_ref, cwu_ref, cbg_ref, cbu_ref, wd_ref, gpost_ref,
                out_ref, acc_ref, carry_ref):
    i = pl.program_id(1)
    f = pl.program_id(2)
    tm = hn_ref.shape[0]
    row = lax.broadcasted_iota(jnp.int32, (tm, wg_ref.shape[1]), 0)

    @pl.when(f == 0)
    def _():
        acc_ref[...] = jnp.zeros_like(acc_ref)

    @pl.when(i == 0)
    def _():
        carry_ref[f] = jnp.zeros(carry_ref.shape[1:], f32)

    hn = hn_ref[...]

    def conv(up, w_ref, b_ref, tail):
        m1 = jnp.where(row == 0, tail[7:8, :], pltpu.roll(up, 1, 0))
        m2 = jnp.where(row == 0, tail[6:7, :], jnp.where(row == 1, tail[7:8, :], pltpu.roll(up, 2, 0)))
        return b_ref[...] + m2 * w_ref[0:1, :] + m1 * w_ref[1:2, :] + up * w_ref[2:3, :]

    up_g = _dot(hn, wg_ref[...])
    up_u = _dot(hn, wu_ref[...])
    g = conv(up_g, cwg_ref, cbg_ref, carry_ref[f, 0])
    u = conv(up_u, cwu_ref, cbu_ref, carry_ref[f, 1])
    carry_ref[f, 0] = up_g[tm - 8:tm, :]
    carry_ref[f, 1] = up_u[tm - 8:tm, :]
    acc_ref[...] += _dot((_silu(g) * u).astype(bf16), wd_ref[...])

    @pl.when(f == pl.num_programs(2) - 1)
    def _():
        out_ref[...] = h_ref[...] + _rms(acc_ref[...], gpost_ref[...])


def _ffn(hn, h, w_up, conv_w, conv_b, w_down, g_post, batch, seq):
    t, d = h.shape
    tm = min(TM_FFN, seq)
    n_seq = seq // tm
    n_f = D_FF // TF
    row = lambda b, i, f: (b * n_seq + i, 0)
    return pl.pallas_call(
        _ffn_kernel,
        grid=(batch, n_seq, n_f),
        in_specs=[pl.BlockSpec((tm, d), row), pl.BlockSpec((tm, d), row),
                  pl.BlockSpec((d, TF), lambda b, i, f: (0, f)),
                  pl.BlockSpec((d, TF), lambda b, i, f: (0, n_f + f)),
                  pl.BlockSpec((CONV_WIDTH, TF), lambda b, i, f: (0, f)),
                  pl.BlockSpec((CONV_WIDTH, TF), lambda b, i, f: (0, n_f + f)),
                  pl.BlockSpec((1, TF), lambda b, i, f: (0, f)),
                  pl.BlockSpec((1, TF), lambda b, i, f: (0, n_f + f)),
                  pl.BlockSpec((TF, d), lambda b, i, f: (f, 0)),
                  pl.BlockSpec((1, d), lambda b, i, f: (0, 0))],
        out_specs=pl.BlockSpec((tm, d), row),
        out_shape=jax.ShapeDtypeStruct((t, d), f32),
        scratch_shapes=[pltpu.VMEM((tm, d), f32), pltpu.VMEM((n_f, 2, 8, TF), f32)],
        compiler_params=pltpu.CompilerParams(dimension_semantics=("arbitrary", "arbitrary", "arbitrary"),
                                             vmem_limit_bytes=VMEM_LIMIT),
        name="ffn",
    )(hn, h, w_up, w_up, conv_w, conv_w, conv_b, conv_b, w_down, g_post)


def _relayout_w_in(w):
    d = w.shape[0]
    sizes = (GROUP, GROUP, GROUP, GROUP, GROUP, HEAD_DIM, HEAD_DIM, IDX_HEADS * IDX_DIM, IDX_DIM, IDX_HEADS)
    parts, off = [], 0
    for n in sizes:
        parts.append(w[:, off:off + n])
        off += n
    rq, rk, rv, rg, aq, ak, av, iq, ik, iw = parts
    pad = jnp.zeros((d, LANES - IDX_HEADS), w.dtype)
    return jnp.concatenate([rq, rk, aq, iq, rv, rg, ak, ak, av, av, ik, ik, iw, pad], axis=1).astype(bf16)


def _rope_tables(seq):
    half = HEAD_DIM // 2
    inv = ROPE_THETA ** (-jnp.arange(0, HEAD_DIM, 2, dtype=f32) / HEAD_DIM)
    ang = jnp.arange(seq, dtype=f32)[:, None] * inv[None, :]
    cos, sin = jnp.cos(ang), jnp.sin(ang)
    cos_t = jnp.tile(cos, (1, LANES // half))
    sin_t = jnp.tile(jnp.concatenate([-sin, sin], axis=1), (1, PAIR))
    return cos_t, sin_t


def kernel(x, mix_norm_pre, mix_norm_post, w_in, w_out, ffn_norm_pre, ffn_norm_post,
           w_up, conv_w, conv_b, w_down):
    batch, seq, d = x.shape
    depth = w_in.shape[0]
    cos_t, sin_t = _rope_tables(seq)
    h = x.reshape(batch * seq, d)
    for l in range(depth):
        rq, rk, aq, iq, rv, rg, kk, vv, ik, iw = _inproj(
            h, mix_norm_pre[l][None, :], _relayout_w_in(w_in[l]), cos_t, sin_t, seq)
        ret = _retention(rq, rk, rv, rg, batch, seq)
        att = _dsa(aq, iq, iw, kk, vv, ik, batch, seq)
        h, hn = _outproj(ret, att, h, w_out[l].astype(bf16), mix_norm_post[l][None, :],
                         ffn_norm_pre[l][None, :], seq)
        h = _ffn(hn, h, w_up[l].astype(bf16), conv_w[l], conv_b[l][None, :], w_down[l].astype(bf16),
                 ffn_norm_post[l][None, :], batch, seq)
    return h.reshape(batch, seq, d)
```

```python
import functools

import jax
import jax.numpy as jnp
from jax import lax
from jax.experimental import pallas as pl
from jax.experimental.pallas import tpu as pltpu

HEAD_DIM = 64
RET_HEADS = 8
DSA_HEADS = 8
IDX_HEADS = 8
IDX_DIM = 64
TOPK_MAX = 256
D_FF = 2816
CONV_WIDTH = 3
ROPE_THETA = 10000.0
EPS = 1e-6

LANES = 128
SUBLANES = 8
PAIR = LANES // HEAD_DIM
N_PAIRS = RET_HEADS // PAIR
GROUP = RET_HEADS * HEAD_DIM

_OFF_RQ, _OFF_RK, _OFF_AQ, _OFF_IQ, _OFF_RV, _OFF_RG = (i * GROUP for i in range(6))
_OFF_KK = 6 * GROUP
_OFF_VV = _OFF_KK + LANES
_OFF_IK = _OFF_VV + LANES
_OFF_IW = _OFF_IK + LANES
_W_TOTAL = _OFF_IW + LANES

TM_PROJ = 512
RET_CHUNK = 256
TQ = 256
KC = 512
TM_FFN = 1024
TF = 256
VMEM_LIMIT = 48 * 1024 * 1024

NEG_BIG = -1e30
INT_MIN = -2 ** 31
KEY_NEG_INF = -2139095041
N_COUNT_ACC = 4
N_REFINE = 12
SUM_ROWS = 16
LOG2E = 1.4426950408889634

f32 = jnp.float32
bf16 = jnp.bfloat16


def _dot(a, b):
    return jnp.dot(a, b, preferred_element_type=f32)


def _dot_nt(a, b):
    return lax.dot_general(a, b, (((1,), (1,)), ((), ())), preferred_element_type=f32)


def _dot_tn(a, b):
    return lax.dot_general(a, b, (((0,), (0,)), ((), ())), preferred_element_type=f32)


def _rms(x, g):
    return x * lax.rsqrt(jnp.mean(x * x, axis=-1, keepdims=True) + EPS) * g


def _silu(x):
    return x * (1.0 / (1.0 + jnp.exp(-x)))


def _inproj_kernel(x_ref, g_ref, w_ref, ch_ref, sh_ref,
                   rq_ref, rk_ref, aq_ref, iq_ref, rv_ref, rg_ref, kk_ref, vv_ref, ik_ref, iw_ref):
    tm = x_ref.shape[0]
    hn = _rms(x_ref[...], g_ref[...]).astype(bf16)
    cos = ch_ref[...]
    sin = sh_ref[...]
    lane = lax.broadcasted_iota(jnp.int32, (tm, LANES), 1)
    first_half = (lane % HEAD_DIM) < (HEAD_DIM // 2)

    def proj(off, width):
        return _dot(hn, w_ref[:, off:off + width])

    def rope(p):
        swapped = jnp.where(first_half, pltpu.roll(p, LANES - HEAD_DIM // 2, 1),
                            pltpu.roll(p, HEAD_DIM // 2, 1))
        return p * cos + swapped * sin

    def roped_group(off, out_ref, scale):
        p = proj(off, GROUP)
        for b in range(GROUP // LANES):
            r = rope(p[:, b * LANES:(b + 1) * LANES])
            if scale != 1.0:
                r = r * scale
            out_ref[:, b * LANES:(b + 1) * LANES] = r.astype(out_ref.dtype)

    roped_group(_OFF_RQ, rq_ref, 1.0)
    roped_group(_OFF_RK, rk_ref, HEAD_DIM ** -0.5)
    roped_group(_OFF_AQ, aq_ref, HEAD_DIM ** -0.5 * LOG2E)
    roped_group(_OFF_IQ, iq_ref, 1.0)
    rv_ref[...] = proj(_OFF_RV, GROUP).astype(bf16)
    rg_ref[...] = proj(_OFF_RG, GROUP)
    small = proj(_OFF_KK, 4 * LANES)
    kk_ref[...] = rope(small[:, 0:LANES]).astype(bf16)
    vv_ref[...] = small[:, LANES:2 * LANES].astype(bf16)
    ik_ref[...] = rope(small[:, 2 * LANES:3 * LANES]).astype(bf16)
    iw_ref[...] = small[:, 3 * LANES:4 * LANES] * (IDX_HEADS ** -0.5 * IDX_DIM ** -0.5)


def _inproj(x2, g, w, cos_t, sin_t, seq):
    t, d = x2.shape
    tm = min(TM_PROJ, seq)
    n_seq = seq // tm
    row = lambda i: (i, 0)
    fixed = lambda i: (0, 0)
    pos = lambda i: (i % n_seq, 0)
    wide = lambda dt: jax.ShapeDtypeStruct((t, GROUP), dt)
    narrow = lambda dt: jax.ShapeDtypeStruct((t, LANES), dt)
    return pl.pallas_call(
        _inproj_kernel,
        grid=(t // tm,),
        in_specs=[pl.BlockSpec((tm, d), row), pl.BlockSpec((1, d), fixed),
                  pl.BlockSpec((d, _W_TOTAL), fixed),
                  pl.BlockSpec((tm, LANES), pos), pl.BlockSpec((tm, LANES), pos)],
        out_specs=[pl.BlockSpec((tm, GROUP), row)] * 6 + [pl.BlockSpec((tm, LANES), row)] * 4,
        out_shape=[wide(bf16), wide(bf16), wide(bf16), wide(bf16), wide(bf16), wide(f32),
                   narrow(bf16), narrow(bf16), narrow(bf16), narrow(f32)],
        compiler_params=pltpu.CompilerParams(dimension_semantics=("arbitrary",),
                                             vmem_limit_bytes=VMEM_LIMIT),
        name="in_proj",
    )(x2, g, w, cos_t, sin_t)


def _retention_kernel(rq_ref, rk_ref, rv_ref, rg_ref, decay_ref, zeta_ref, xi_ref, cd_ref,
                      out_ref, state_ref):
    c = rq_ref.shape[0]

    @pl.when(pl.program_id(1) == 0)
    def _():
        state_ref[...] = jnp.zeros_like(state_ref)

    lane = lax.broadcasted_iota(jnp.int32, (c, LANES), 1)
    even = lane < HEAD_DIM
    r_i = lax.broadcasted_iota(jnp.int32, (LANES, LANES), 0)
    c_i = lax.broadcasted_iota(jnp.int32, (LANES, LANES), 1)
    same_head = (r_i < HEAD_DIM) == (c_i < HEAD_DIM)
    ones_bd = jnp.where(same_head, 1.0, 0.0).astype(bf16)

    for p in range(N_PAIRS):
        sl = slice(p * LANES, (p + 1) * LANES)
        q2 = rq_ref[:, sl]
        k2 = rk_ref[:, sl]
        v2 = rv_ref[:, sl]
        zero = jnp.zeros_like(q2)
        s_e = _dot_nt(jnp.where(even, q2, zero), k2) * decay_ref[2 * p]
        s_o = _dot_nt(jnp.where(even, zero, q2), k2) * decay_ref[2 * p + 1]
        inner = jnp.where(even, _dot(s_e.astype(bf16), v2), _dot(s_o.astype(bf16), v2))
        r_prev = state_ref[p]
        cross = _dot(q2, r_prev.astype(bf16)) * xi_ref[p]
        o = inner + cross
        kz = (k2.astype(f32) * zeta_ref[p]).astype(bf16)
        s_new = jnp.where(same_head, _dot_tn(kz, v2), 0.0)
        state_ref[p] = cd_ref[p] * r_prev + s_new
        sq = o * o
        hi = sq.astype(bf16)
        lo = (sq - hi.astype(f32)).astype(bf16)
        ms = (_dot(hi, ones_bd) + _dot(lo, ones_bd)) * (1.0 / HEAD_DIM)
        o = o * lax.rsqrt(ms + EPS)
        out_ref[:, sl] = (o * _silu(rg_ref[:, sl])).astype(out_ref.dtype)


def _retention_tables(c):
    h = jnp.arange(RET_HEADS, dtype=f32)
    log_g = jnp.log(1.0 - 2.0 ** (-5.0 - h))
    pos = jnp.arange(c, dtype=f32)
    diff = pos[:, None] - pos[None, :]
    decay = jnp.where(diff >= 0, jnp.exp(log_g[:, None, None] * jnp.maximum(diff, 0.0)), 0.0)
    zeta = jnp.exp(log_g[:, None] * (c - 1.0 - pos)[None, :])
    xi = jnp.exp(log_g[:, None] * (pos + 1.0)[None, :])
    cd = jnp.exp(log_g * c)

    def pair_lanes(a):
        a = a.reshape(N_PAIRS, PAIR, c)
        return jnp.repeat(jnp.transpose(a, (0, 2, 1)), HEAD_DIM, axis=2)

    cd2 = jnp.repeat(cd.reshape(N_PAIRS, 1, PAIR), HEAD_DIM, axis=2)
    return decay, pair_lanes(zeta), pair_lanes(xi), cd2


def _retention(rq, rk, rv, rg, batch, seq):
    t = rq.shape[0]
    c = min(RET_CHUNK, seq)
    n = seq // c
    decay, zeta, xi, cd = _retention_tables(c)
    row = lambda b, i: (b * n + i, 0)
    fix3 = lambda b, i: (0, 0, 0)
    return pl.pallas_call(
        _retention_kernel,
        grid=(batch, n),
        in_specs=[pl.BlockSpec((c, GROUP), row)] * 4 + [
            pl.BlockSpec((RET_HEADS, c, c), fix3), pl.BlockSpec((N_PAIRS, c, LANES), fix3),
            pl.BlockSpec((N_PAIRS, c, LANES), fix3), pl.BlockSpec((N_PAIRS, 1, LANES), fix3)],
        out_specs=pl.BlockSpec((c, GROUP), row),
        out_shape=jax.ShapeDtypeStruct((t, GROUP), bf16),
        scratch_shapes=[pltpu.VMEM((N_PAIRS, LANES, LANES), f32)],
        compiler_params=pltpu.CompilerParams(dimension_semantics=("arbitrary", "arbitrary"),
                                             vmem_limit_bytes=VMEM_LIMIT),
        name="retention",
    )(rq, rk, rv, rg, decay, zeta, xi, cd)


def _key_to_float(key):
    bits = jnp.where(key >= 0, key, key ^ jnp.int32(0x7FFFFFFF))
    return lax.bitcast_convert_type(bits, f32)


def _dsa_kernel(k_sel, aq_ref, iq_ref, iw_ref, kk_ref, vv_ref, ik_ref, out_ref,
                lhs_a, lhs_i, xs, bias, prod_a, prod_b, p_ref, m_ref, alpha_ref, acc_ref, vaug, earlier):
    tq = aq_ref.shape[0]
    n_kc, kc, _ = xs.shape
    j = pl.program_id(1)
    n_chunks = ((j + 1) * tq + kc - 1) // kc
    kf = float(k_sel)

    @pl.when((pl.program_id(0) == 0) & (j == 0))
    def _():
        r_i = lax.broadcasted_iota(jnp.int32, (kc, kc), 0)
        c_i = lax.broadcasted_iota(jnp.int32, (kc, kc), 1)
        earlier[...] = jnp.where(c_i < r_i, 1.0, 0.0).astype(bf16)

    @pl.when(j == 0)
    def _():
        row = lax.broadcasted_iota(jnp.int32, (LANES, kc), 0)
        for c in range(n_kc):
            vt = vv_ref[c * kc:(c + 1) * kc, :].astype(f32).T
            vaug[c] = jnp.where(row < HEAD_DIM, vt, 1.0)[0:HEAD_DIM + SUM_ROWS, :].astype(bf16)

    lane = lax.broadcasted_iota(jnp.int32, (tq, LANES), 1)
    even = lane < HEAD_DIM
    for p in range(N_PAIRS):
        sl = slice(p * LANES, (p + 1) * LANES)
        a2 = aq_ref[:, sl]
        i2 = iq_ref[:, sl]
        zero = jnp.zeros_like(a2)
        lhs_a[(2 * p) * tq:(2 * p + 1) * tq, :] = jnp.where(even, a2, zero)
        lhs_a[(2 * p + 1) * tq:(2 * p + 2) * tq, :] = jnp.where(even, zero, a2)
        lhs_i[(2 * p) * tq:(2 * p + 1) * tq, :] = jnp.where(even, i2, zero)
        lhs_i[(2 * p + 1) * tq:(2 * p + 2) * tq, :] = jnp.where(even, zero, i2)
    w_heads = iw_ref[...].T[0:IDX_HEADS, :]

    key_row = lax.broadcasted_iota(jnp.int32, (kc, tq), 0)
    q_pos = j * tq + lax.broadcasted_iota(jnp.int32, (kc, tq), 1)

    def pipelined(key_ref, lhs_ref, work):
        def fill(dst, c):
            start = pl.multiple_of(jnp.minimum(c, n_chunks - 1) * kc, kc)
            dst[...] = _dot_nt(key_ref[pl.ds(start, kc), :], lhs_ref[...])

        def one_chunk(c, _):
            fill(prod_a, c)
            work(c, prod_a)
            return 0

        lax.fori_loop(0, n_chunks, one_chunk, 0)

    def score_chunk(c, prod):
        isc = jnp.zeros((kc, tq), f32)
        for h in range(IDX_HEADS):
            isc = isc + w_heads[h:h + 1, :] * jnp.maximum(prod[:, h * tq:(h + 1) * tq], 0.0)
        xs[c] = jnp.where(c * kc + key_row <= q_pos, isc, -jnp.inf)

    pipelined(ik_ref, lhs_i, score_chunk)

    def count_where(pred):
        def body(c, accs):
            x = xs[c]
            accs = list(accs)
            for r in range(kc // SUBLANES):
                hit = jnp.where(pred(x[r * SUBLANES:(r + 1) * SUBLANES, :]), 1.0, 0.0)
                accs[r % N_COUNT_ACC] = accs[r % N_COUNT_ACC] + hit
            return tuple(accs)
        zero = jnp.zeros((SUBLANES, tq), f32)
        accs = lax.fori_loop(0, n_chunks, body, (zero,) * N_COUNT_ACC)
        total = functools.reduce(lambda a, b: a + b, accs)
        return jnp.broadcast_to(jnp.sum(total, axis=0, keepdims=True), (SUBLANES, tq))

    def count_ge(cand_f):
        return count_where(lambda x: x >= cand_f)

    def bit_step(i, carry):
        t_key, t_cnt = carry
        cand = t_key + lax.shift_left(jnp.int32(1), jnp.int32(31) - i)
        cnt = count_ge(_key_to_float(cand))
        ok = cnt >= kf
        return jnp.where(ok, cand, t_key), jnp.where(ok, cnt, t_cnt)

    t_key, t_cnt = lax.fori_loop(
        0, 32, bit_step,
        (jnp.full((SUBLANES, tq), INT_MIN, jnp.int32),
         jnp.zeros((SUBLANES, tq), f32) + (n_chunks * kc).astype(f32)))

    t_sel = _key_to_float(jnp.maximum(t_key, KEY_NEG_INF + 1))[0:1, :]

    def bias_chunk(c, _):
        bias[c] = jnp.where(xs[c] >= t_sel, 0.0, NEG_BIG).astype(bias.dtype)
        return 0

    lax.fori_loop(0, n_chunks, bias_chunk, 0)

    real = t_key > KEY_NEG_INF
    tied = (t_cnt > kf) & real

    @pl.when(jnp.max(jnp.where(tied, 1.0, 0.0)) > 0.0)
    def _():
        lo0 = _key_to_float(t_key)
        hi0 = _key_to_float(t_key + 1)
        n_hi0 = count_ge(hi0)
        n_eq = count_where(lambda x: x == lo0)
        unequal = tied & (n_eq != t_cnt - n_hi0)

        def refined(args):
            def refine(_, carry):
                lo, hi, n_hi = carry
                mid = lo + (hi - lo) * 0.5
                cnt = count_ge(mid)
                ok = cnt >= kf
                return jnp.where(ok, mid, lo), jnp.where(ok, hi, mid), jnp.where(ok, n_hi, cnt)
            return lax.fori_loop(0, N_REFINE, refine, args)

        lo8, hi8, n_hi8 = lax.cond(jnp.max(jnp.where(unequal, 1.0, 0.0)) > 0.0, refined, lambda args: args,
                                   (lo0, hi0, n_hi0))
        lo, hi, need = lo8[0:1, :], hi8[0:1, :], kf - n_hi8[0:1, :]
        real_row = real[0:1, :]

        def tie_chunk(c, seen):
            x = xs[c]
            above = jnp.where(x >= hi, 1.0, 0.0)
            tie = jnp.where(x >= lo, 1.0, 0.0) - above
            rank = seen + _dot(earlier[...], tie.astype(bf16))
            sel = above + jnp.where(rank < need, tie, 0.0)
            sel = jnp.where(real_row, sel, jnp.where(x >= t_sel, 1.0, 0.0))
            bias[c] = ((sel - 1.0) * -NEG_BIG).astype(bias.dtype)
            return seen + jnp.sum(tie, axis=0, keepdims=True)

        lax.fori_loop(0, n_chunks, tie_chunk, jnp.zeros((1, tq), f32))

    m_ref[...] = jnp.full(m_ref.shape, -jnp.inf, f32)
    acc_ref[...] = jnp.zeros(acc_ref.shape, f32)

    def attend_chunk(c, prod):
        b = bias[c]
        for h in range(DSA_HEADS):
            hs = slice(h * tq, (h + 1) * tq)
            s = prod[:, hs].astype(bf16) + b
            m_old = m_ref[:, hs]
            m_new = jnp.maximum(m_old, jnp.max(s, axis=0, keepdims=True).astype(f32))
            alpha_ref[:, hs] = jnp.exp2(m_old - m_new)
            p_ref[:, hs] = jnp.exp2(s - m_new.astype(bf16))
            m_ref[:, hs] = m_new
        acc_ref[...] = alpha_ref[...] * acc_ref[...] + _dot(vaug[c], p_ref[...])

    pipelined(kk_ref, lhs_a, attend_chunk)

    for p in range(N_PAIRS):
        halves = []
        for e in range(PAIR):
            hs = slice((PAIR * p + e) * tq, (PAIR * p + e + 1) * tq)
            halves.append(acc_ref[0:HEAD_DIM, hs] * (1.0 / acc_ref[HEAD_DIM:HEAD_DIM + 1, hs]))
        out_ref[:, p * LANES:(p + 1) * LANES] = jnp.concatenate(halves, axis=0).T.astype(out_ref.dtype)


def _dsa(aq, iq, iw, kk, vv, ik, batch, seq):
    t = aq.shape[0]
    tq = min(TQ, seq)
    kc = min(KC, seq)
    nq = seq // tq
    k_sel = min(TOPK_MAX, seq // 4)
    heads = DSA_HEADS
    qrow = lambda b, j: (b * nq + j, 0)
    brow = lambda b, j: (b, 0)
    return pl.pallas_call(
        functools.partial(_dsa_kernel, k_sel),
        grid=(batch, nq),
        in_specs=[pl.BlockSpec((tq, GROUP), qrow), pl.BlockSpec((tq, GROUP), qrow),
                  pl.BlockSpec((tq, LANES), qrow),
                  pl.BlockSpec((seq, LANES), brow), pl.BlockSpec((seq, LANES), brow),
                  pl.BlockSpec((seq, LANES), brow)],
        out_specs=pl.BlockSpec((tq, GROUP), qrow),
        out_shape=jax.ShapeDtypeStruct((t, GROUP), bf16),
        scratch_shapes=[
            pltpu.VMEM((heads * tq, LANES), bf16),
            pltpu.VMEM((heads * tq, LANES), bf16),
            pltpu.VMEM((seq // kc, kc, tq), f32),
            pltpu.VMEM((seq // kc, kc, tq), bf16),
            pltpu.VMEM((kc, heads * tq), f32),
            pltpu.VMEM((kc, heads * tq), f32),
            pltpu.VMEM((kc, heads * tq), bf16),
            pltpu.VMEM((1, heads * tq), f32),
            pltpu.VMEM((1, heads * tq), f32),
            pltpu.VMEM((HEAD_DIM + SUM_ROWS, heads * tq), f32),
            pltpu.VMEM((seq // kc, HEAD_DIM + SUM_ROWS, kc), bf16),
            pltpu.VMEM((kc, kc), bf16),
        ],
        compiler_params=pltpu.CompilerParams(dimension_semantics=("arbitrary", "arbitrary"),
                                             vmem_limit_bytes=VMEM_LIMIT),
        name="dsa",
    )(aq, iq, iw, kk, vv, ik)


def _outproj_kernel(ret_ref, att_ref, x_ref, w_ref, gpost_ref, gpre_ref, h_ref, hn_ref):
    mixed = _dot(ret_ref[...], w_ref[0:GROUP, :]) + _dot(att_ref[...], w_ref[GROUP:2 * GROUP, :])
    h = x_ref[...] + _rms(mixed, gpost_ref[...])
    h_ref[...] = h
    hn_ref[...] = _rms(h, gpre_ref[...]).astype(hn_ref.dtype)


def _outproj(ret, att, x2, w, g_post, g_pre, seq):
    t, d = x2.shape
    tm = min(TM_PROJ, seq)
    row = lambda i: (i, 0)
    fixed = lambda i: (0, 0)
    return pl.pallas_call(
        _outproj_kernel,
        grid=(t // tm,),
        in_specs=[pl.BlockSpec((tm, GROUP), row), pl.BlockSpec((tm, GROUP), row),
                  pl.BlockSpec((tm, d), row), pl.BlockSpec((2 * GROUP, d), fixed),
                  pl.BlockSpec((1, d), fixed), pl.BlockSpec((1, d), fixed)],
        out_specs=[pl.BlockSpec((tm, d), row), pl.BlockSpec((tm, d), row)],
        out_shape=[jax.ShapeDtypeStruct((t, d), f32), jax.ShapeDtypeStruct((t, d), bf16)],
        compiler_params=pltpu.CompilerParams(dimension_semantics=("arbitrary",),
                                             vmem_limit_bytes=VMEM_LIMIT),
        name="out_proj",
    )(ret, att, x2, w, g_post, g_pre)


def _ffn_kernel(hn_ref, h_ref, wg_ref, wu_ref, cwg_ref, cwu_ref, cbg_ref, cbu_ref, wd_ref, gpost_ref,
                out_ref, acc_ref, carry_ref):
    i = pl.program_id(1)
    f = pl.program_id(2)
    tm = hn_ref.shape[0]
    row = lax.broadcasted_iota(jnp.int32, (tm, wg_ref.shape[1]), 0)

    @pl.when(f == 0)
    def _():
        acc_ref[...] = jnp.zeros_like(acc_ref)

    @pl.when(i == 0)
    def _():
        carry_ref[f] = jnp.zeros(carry_ref.shape[1:], f32)

    hn = hn_ref[...]

    def conv(up, w_ref, b_ref, tail):
        m1 = jnp.where(row == 0, tail[7:8, :], pltpu.roll(up, 1, 0))
        m2 = jnp.where(row == 0, tail[6:7, :], jnp.where(row == 1, tail[7:8, :], pltpu.roll(up, 2, 0)))
        return b_ref[...] + m2 * w_ref[0:1, :] + m1 * w_ref[1:2, :] + up * w_ref[2:3, :]

    up_g = _dot(hn, wg_ref[...])
    up_u = _dot(hn, wu_ref[...])
    g = conv(up_g, cwg_ref, cbg_ref, carry_ref[f, 0])
    u = conv(up_u, cwu_ref, cbu_ref, carry_ref[f, 1])
    carry_ref[f, 0] = up_g[tm - 8:tm, :]
    carry_ref[f, 1] = up_u[tm - 8:tm, :]
    acc_ref[...] += _dot((_silu(g) * u).astype(bf16), wd_ref[...])

    @pl.when(f == pl.num_programs(2) - 1)
    def _():
        out_ref[...] = h_ref[...] + _rms(acc_ref[...], gpost_ref[...])


def _ffn(hn, h, w_up, conv_w, conv_b, w_down, g_post, batch, seq):
    t, d = h.shape
    tm = min(TM_FFN, seq)
    n_seq = seq // tm
    n_f = D_FF // TF
    row = lambda b, i, f: (b * n_seq + i, 0)
    return pl.pallas_call(
        _ffn_kernel,
        grid=(batch, n_seq, n_f),
        in_specs=[pl.BlockSpec((tm, d), row), pl.BlockSpec((tm, d), row),
                  pl.BlockSpec((d, TF), lambda b, i, f: (0, f)),
                  pl.BlockSpec((d, TF), lambda b, i, f: (0, n_f + f)),
                  pl.BlockSpec((CONV_WIDTH, TF), lambda b, i, f: (0, f)),
                  pl.BlockSpec((CONV_WIDTH, TF), lambda b, i, f: (0, n_f + f)),
                  pl.BlockSpec((1, TF), lambda b, i, f: (0, f)),
                  pl.BlockSpec((1, TF), lambda b, i, f: (0, n_f + f)),
                  pl.BlockSpec((TF, d), lambda b, i, f: (f, 0)),
                  pl.BlockSpec((1, d), lambda b, i, f: (0, 0))],
        out_specs=pl.BlockSpec((tm, d), row),
        out_shape=jax.ShapeDtypeStruct((t, d), f32),
        scratch_shapes=[pltpu.VMEM((tm, d), f32), pltpu.VMEM((n_f, 2, 8, TF), f32)],
        compiler_params=pltpu.CompilerParams(dimension_semantics=("arbitrary", "arbitrary", "arbitrary"),
                                             vmem_limit_bytes=VMEM_LIMIT),
        name="ffn",
    )(hn, h, w_up, w_up, conv_w, conv_w, conv_b, conv_b, w_down, g_post)


def _relayout_w_in(w):
    d = w.shape[0]
    sizes = (GROUP, GROUP, GROUP, GROUP, GROUP, HEAD_DIM, HEAD_DIM, IDX_HEADS * IDX_DIM, IDX_DIM, IDX_HEADS)
    parts, off = [], 0
    for n in sizes:
        parts.append(w[:, off:off + n])
        off += n
    rq, rk, rv, rg, aq, ak, av, iq, ik, iw = parts
    pad = jnp.zeros((d, LANES - IDX_HEADS), w.dtype)
    return jnp.concatenate([rq, rk, aq, iq, rv, rg, ak, ak, av, av, ik, ik, iw, pad], axis=1).astype(bf16)


def _rope_tables(seq):
    half = HEAD_DIM // 2
    inv = ROPE_THETA ** (-jnp.arange(0, HEAD_DIM, 2, dtype=f32) / HEAD_DIM)
    ang = jnp.arange(seq, dtype=f32)[:, None] * inv[None, :]
    cos, sin = jnp.cos(ang), jnp.sin(ang)
    cos_t = jnp.tile(cos, (1, LANES // half))
    sin_t = jnp.tile(jnp.concatenate([-sin, sin], axis=1), (1, PAIR))
    return cos_t, sin_t


def kernel(x, mix_norm_pre, mix_norm_post, w_in, w_out, ffn_norm_pre, ffn_norm_post,
           w_up, conv_w, conv_b, w_down):
    batch, seq, d = x.shape
    depth = w_in.shape[0]
    cos_t, sin_t = _rope_tables(seq)
    h = x.reshape(batch * seq, d)
    for l in range(depth):
        rq, rk, aq, iq, rv, rg, kk, vv, ik, iw = _inproj(
            h, mix_norm_pre[l][None, :], _relayout_w_in(w_in[l]), cos_t, sin_t, seq)
        ret = _retention(rq, rk, rv, rg, batch, seq)
        att = _dsa(aq, iq, iw, kk, vv, ik, batch, seq)
        h, hn = _outproj(ret, att, h, w_out[l].astype(bf16), mix_norm_post[l][None, :],
                         ffn_norm_pre[l][None, :], seq)
        h = _ffn(hn, h, w_up[l].astype(bf16), conv_w[l], conv_b[l][None, :], w_down[l].astype(bf16),
                 ffn_norm_post[l][None, :], batch, seq)
    return h.reshape(batch, seq, d)
```

```python
import functools

import jax
import jax.numpy as jnp
from jax import lax
from jax.experimental import pallas as pl
from jax.experimental.pallas import tpu as pltpu

HEAD_DIM = 64
RET_HEADS = 8
DSA_HEADS = 8
IDX_HEADS = 8
IDX_DIM = 64
TOPK_MAX = 256
D_FF = 2816
CONV_WIDTH = 3
ROPE_THETA = 10000.0
EPS = 1e-6

LANES = 128
SUBLANES = 8
PAIR = LANES // HEAD_DIM
N_PAIRS = RET_HEADS // PAIR
GROUP = RET_HEADS * HEAD_DIM

_OFF_RQ, _OFF_RK, _OFF_AQ, _OFF_IQ, _OFF_RV, _OFF_RG = (i * GROUP for i in range(6))
_OFF_KK = 6 * GROUP
_OFF_VV = _OFF_KK + LANES
_OFF_IK = _OFF_VV + LANES
_OFF_IW = _OFF_IK + LANES
_W_TOTAL = _OFF_IW + LANES

TM_PROJ = 512
RET_CHUNK = 256
TQ = 256
KC = 512
TM_FFN = 1024
TF = 1408
VMEM_LIMIT = 48 * 1024 * 1024

NEG_BIG = -1e30
INT_MIN = -2 ** 31
KEY_NEG_INF = -2139095041
N_COUNT_ACC = 4
N_REFINE = 12
SUM_ROWS = 16
LOG2E = 1.4426950408889634

f32 = jnp.float32
bf16 = jnp.bfloat16


def _dot(a, b):
    return jnp.dot(a, b, preferred_element_type=f32)


def _dot_nt(a, b):
    return lax.dot_general(a, b, (((1,), (1,)), ((), ())), preferred_element_type=f32)


def _dot_tn(a, b):
    return lax.dot_general(a, b, (((0,), (0,)), ((), ())), preferred_element_type=f32)


def _rms(x, g):
    return x * lax.rsqrt(jnp.mean(x * x, axis=-1, keepdims=True) + EPS) * g


def _silu(x):
    return x * (1.0 / (1.0 + jnp.exp(-x)))


def _inproj_kernel(x_ref, g_ref, w_ref, ch_ref, sh_ref,
                   rq_ref, rk_ref, aq_ref, iq_ref, rv_ref, rg_ref, kk_ref, vv_ref, ik_ref, iw_ref):
    tm = x_ref.shape[0]
    hn = _rms(x_ref[...], g_ref[...]).astype(bf16)
    cos = ch_ref[...]
    sin = sh_ref[...]
    lane = lax.broadcasted_iota(jnp.int32, (tm, LANES), 1)
    first_half = (lane % HEAD_DIM) < (HEAD_DIM // 2)

    def proj(off, width):
        return _dot(hn, w_ref[:, off:off + width])

    def rope(p):
        swapped = jnp.where(first_half, pltpu.roll(p, LANES - HEAD_DIM // 2, 1),
                            pltpu.roll(p, HEAD_DIM // 2, 1))
        return p * cos + swapped * sin

    def roped_group(off, out_ref, scale):
        p = proj(off, GROUP)
        for b in range(GROUP // LANES):
            r = rope(p[:, b * LANES:(b + 1) * LANES])
            if scale != 1.0:
                r = r * scale
            out_ref[:, b * LANES:(b + 1) * LANES] = r.astype(out_ref.dtype)

    roped_group(_OFF_RQ, rq_ref, 1.0)
    roped_group(_OFF_RK, rk_ref, HEAD_DIM ** -0.5)
    roped_group(_OFF_AQ, aq_ref, HEAD_DIM ** -0.5 * LOG2E)
    roped_group(_OFF_IQ, iq_ref, 1.0)
    rv_ref[...] = proj(_OFF_RV, GROUP).astype(bf16)
    rg_ref[...] = proj(_OFF_RG, GROUP)
    small = proj(_OFF_KK, 4 * LANES)
    kk_ref[...] = rope(small[:, 0:LANES]).astype(bf16)
    vv_ref[...] = small[:, LANES:2 * LANES].astype(bf16)
    ik_ref[...] = rope(small[:, 2 * LANES:3 * LANES]).astype(bf16)
    iw_ref[...] = small[:, 3 * LANES:4 * LANES] * (IDX_HEADS ** -0.5 * IDX_DIM ** -0.5)


def _inproj(x2, g, w, cos_t, sin_t, seq):
    t, d = x2.shape
    tm = min(TM_PROJ, seq)
    n_seq = seq // tm
    row = lambda i: (i, 0)
    fixed = lambda i: (0, 0)
    pos = lambda i: (i % n_seq, 0)
    wide = lambda dt: jax.ShapeDtypeStruct((t, GROUP), dt)
    narrow = lambda dt: jax.ShapeDtypeStruct((t, LANES), dt)
    return pl.pallas_call(
        _inproj_kernel,
        grid=(t // tm,),
        in_specs=[pl.BlockSpec((tm, d), row), pl.BlockSpec((1, d), fixed),
                  pl.BlockSpec((d, _W_TOTAL), fixed),
                  pl.BlockSpec((tm, LANES), pos), pl.BlockSpec((tm, LANES), pos)],
        out_specs=[pl.BlockSpec((tm, GROUP), row)] * 6 + [pl.BlockSpec((tm, LANES), row)] * 4,
        out_shape=[wide(bf16), wide(bf16), wide(bf16), wide(bf16), wide(bf16), wide(f32),
                   narrow(bf16), narrow(bf16), narrow(bf16), narrow(f32)],
        compiler_params=pltpu.CompilerParams(dimension_semantics=("arbitrary",),
                                             vmem_limit_bytes=VMEM_LIMIT),
        name="in_proj",
    )(x2, g, w, cos_t, sin_t)


def _retention_kernel(rq_ref, rk_ref, rv_ref, rg_ref, decay_ref, zeta_ref, xi_ref, cd_ref,
                      out_ref, state_ref):
    c = rq_ref.shape[0]

    @pl.when(pl.program_id(1) == 0)
    def _():
        state_ref[...] = jnp.zeros_like(state_ref)

    lane = lax.broadcasted_iota(jnp.int32, (c, LANES), 1)
    even = lane < HEAD_DIM
    r_i = lax.broadcasted_iota(jnp.int32, (LANES, LANES), 0)
    c_i = lax.broadcasted_iota(jnp.int32, (LANES, LANES), 1)
    same_head = (r_i < HEAD_DIM) == (c_i < HEAD_DIM)
    ones_bd = jnp.where(same_head, 1.0, 0.0).astype(bf16)

    for p in range(N_PAIRS):
        sl = slice(p * LANES, (p + 1) * LANES)
        q2 = rq_ref[:, sl]
        k2 = rk_ref[:, sl]
        v2 = rv_ref[:, sl]
        zero = jnp.zeros_like(q2)
        s_e = _dot_nt(jnp.where(even, q2, zero), k2) * decay_ref[2 * p]
        s_o = _dot_nt(jnp.where(even, zero, q2), k2) * decay_ref[2 * p + 1]
        inner = jnp.where(even, _dot(s_e.astype(bf16), v2), _dot(s_o.astype(bf16), v2))
        r_prev = state_ref[p]
        cross = _dot(q2, r_prev.astype(bf16)) * xi_ref[p]
        o = inner + cross
        kz = (k2.astype(f32) * zeta_ref[p]).astype(bf16)
        s_new = jnp.where(same_head, _dot_tn(kz, v2), 0.0)
        state_ref[p] = cd_ref[p] * r_prev + s_new
        sq = o * o
        hi = sq.astype(bf16)
        lo = (sq - hi.astype(f32)).astype(bf16)
        ms = (_dot(hi, ones_bd) + _dot(lo, ones_bd)) * (1.0 / HEAD_DIM)
        o = o * lax.rsqrt(ms + EPS)
        out_ref[:, sl] = (o * _silu(rg_ref[:, sl])).astype(out_ref.dtype)


def _retention_tables(c):
    h = jnp.arange(RET_HEADS, dtype=f32)
    log_g = jnp.log(1.0 - 2.0 ** (-5.0 - h))
    pos = jnp.arange(c, dtype=f32)
    diff = pos[:, None] - pos[None, :]
    decay = jnp.where(diff >= 0, jnp.exp(log_g[:, None, None] * jnp.maximum(diff, 0.0)), 0.0)
    zeta = jnp.exp(log_g[:, None] * (c - 1.0 - pos)[None, :])
    xi = jnp.exp(log_g[:, None] * (pos + 1.0)[None, :])
    cd = jnp.exp(log_g * c)

    def pair_lanes(a):
        a = a.reshape(N_PAIRS, PAIR, c)
        return jnp.repeat(jnp.transpose(a, (0, 2, 1)), HEAD_DIM, axis=2)

    cd2 = jnp.repeat(cd.reshape(N_PAIRS, 1, PAIR), HEAD_DIM, axis=2)
    return decay, pair_lanes(zeta), pair_lanes(xi), cd2


def _retention(rq, rk, rv, rg, batch, seq):
    t = rq.shape[0]
    c = min(RET_CHUNK, seq)
    n = seq // c
    decay, zeta, xi, cd = _retention_tables(c)
    row = lambda b, i: (b * n + i, 0)
    fix3 = lambda b, i: (0, 0, 0)
    return pl.pallas_call(
        _retention_kernel,
        grid=(batch, n),
        in_specs=[pl.BlockSpec((c, GROUP), row)] * 4 + [
            pl.BlockSpec((RET_HEADS, c, c), fix3), pl.BlockSpec((N_PAIRS, c, LANES), fix3),
            pl.BlockSpec((N_PAIRS, c, LANES), fix3), pl.BlockSpec((N_PAIRS, 1, LANES), fix3)],
        out_specs=pl.BlockSpec((c, GROUP), row),
        out_shape=jax.ShapeDtypeStruct((t, GROUP), bf16),
        scratch_shapes=[pltpu.VMEM((N_PAIRS, LANES, LANES), f32)],
        compiler_params=pltpu.CompilerParams(dimension_semantics=("arbitrary", "arbitrary"),
                                             vmem_limit_bytes=VMEM_LIMIT),
        name="retention",
    )(rq, rk, rv, rg, decay, zeta, xi, cd)


def _key_to_float(key):
    bits = jnp.where(key >= 0, key, key ^ jnp.int32(0x7FFFFFFF))
    return lax.bitcast_convert_type(bits, f32)


def _dsa_kernel(k_sel, aq_ref, iq_ref, iw_ref, kk_ref, vv_ref, ik_ref, out_ref,
                lhs_a, lhs_i, xs, bias, prod_a, prod_b, p_ref, m_ref, alpha_ref, acc_ref, vaug, earlier):
    tq = aq_ref.shape[0]
    n_kc, kc, _ = xs.shape
    j = pl.program_id(1)
    n_chunks = ((j + 1) * tq + kc - 1) // kc
    kf = float(k_sel)

    @pl.when((pl.program_id(0) == 0) & (j == 0))
    def _():
        r_i = lax.broadcasted_iota(jnp.int32, (kc, kc), 0)
        c_i = lax.broadcasted_iota(jnp.int32, (kc, kc), 1)
        earlier[...] = jnp.where(c_i < r_i, 1.0, 0.0).astype(bf16)

    @pl.when(j == 0)
    def _():
        row = lax.broadcasted_iota(jnp.int32, (LANES, kc), 0)
        for c in range(n_kc):
            vt = vv_ref[c * kc:(c + 1) * kc, :].astype(f32).T
            vaug[c] = jnp.where(row < HEAD_DIM, vt, 1.0)[0:HEAD_DIM + SUM_ROWS, :].astype(bf16)

    lane = lax.broadcasted_iota(jnp.int32, (tq, LANES), 1)
    even = lane < HEAD_DIM
    for p in range(N_PAIRS):
        sl = slice(p * LANES, (p + 1) * LANES)
        a2 = aq_ref[:, sl]
        i2 = iq_ref[:, sl]
        zero = jnp.zeros_like(a2)
        lhs_a[(2 * p) * tq:(2 * p + 1) * tq, :] = jnp.where(even, a2, zero)
        lhs_a[(2 * p + 1) * tq:(2 * p + 2) * tq, :] = jnp.where(even, zero, a2)
        lhs_i[(2 * p) * tq:(2 * p + 1) * tq, :] = jnp.where(even, i2, zero)
        lhs_i[(2 * p + 1) * tq:(2 * p + 2) * tq, :] = jnp.where(even, zero, i2)
    w_heads = iw_ref[...].T[0:IDX_HEADS, :]

    key_row = lax.broadcasted_iota(jnp.int32, (kc, tq), 0)
    q_pos = j * tq + lax.broadcasted_iota(jnp.int32, (kc, tq), 1)

    def pipelined(key_ref, lhs_ref, work):
        def fill(dst, c):
            start = pl.multiple_of(jnp.minimum(c, n_chunks - 1) * kc, kc)
            dst[...] = _dot_nt(key_ref[pl.ds(start, kc), :], lhs_ref[...])

        def one_chunk(c, _):
            fill(prod_a, c)
            work(c, prod_a)
            return 0

        lax.fori_loop(0, n_chunks, one_chunk, 0)

    def score_chunk(c, prod):
        isc = jnp.zeros((kc, tq), f32)
        for h in range(IDX_HEADS):
            isc = isc + w_heads[h:h + 1, :] * jnp.maximum(prod[:, h * tq:(h + 1) * tq], 0.0)
        xs[c] = jnp.where(c * kc + key_row <= q_pos, isc, -jnp.inf)

    pipelined(ik_ref, lhs_i, score_chunk)

    def count_where(pred):
        def body(c, accs):
            x = xs[c]
            accs = list(accs)
            for r in range(kc // SUBLANES):
                hit = jnp.where(pred(x[r * SUBLANES:(r + 1) * SUBLANES, :]), 1.0, 0.0)
                accs[r % N_COUNT_ACC] = accs[r % N_COUNT_ACC] + hit
            return tuple(accs)
        zero = jnp.zeros((SUBLANES, tq), f32)
        accs = lax.fori_loop(0, n_chunks, body, (zero,) * N_COUNT_ACC)
        total = functools.reduce(lambda a, b: a + b, accs)
        return jnp.broadcast_to(jnp.sum(total, axis=0, keepdims=True), (SUBLANES, tq))

    def count_ge(cand_f):
        return count_where(lambda x: x >= cand_f)

    def bit_step(i, carry):
        t_key, t_cnt = carry
        cand = t_key + lax.shift_left(jnp.int32(1), jnp.int32(31) - i)
        cnt = count_ge(_key_to_float(cand))
        ok = cnt >= kf
        return jnp.where(ok, cand, t_key), jnp.where(ok, cnt, t_cnt)

    t_key, t_cnt = lax.fori_loop(
        0, 32, bit_step,
        (jnp.full((SUBLANES, tq), INT_MIN, jnp.int32),
         jnp.zeros((SUBLANES, tq), f32) + (n_chunks * kc).astype(f32)))

    t_sel = _key_to_float(jnp.maximum(t_key, KEY_NEG_INF + 1))[0:1, :]

    def bias_chunk(c, _):
        bias[c] = jnp.where(xs[c] >= t_sel, 0.0, NEG_BIG).astype(bias.dtype)
        return 0

    lax.fori_loop(0, n_chunks, bias_chunk, 0)

    real = t_key > KEY_NEG_INF
    tied = (t_cnt > kf) & real

    @pl.when(jnp.max(jnp.where(tied, 1.0, 0.0)) > 0.0)
    def _():
        lo0 = _key_to_float(t_key)
        hi0 = _key_to_float(t_key + 1)
        n_hi0 = count_ge(hi0)
        n_eq = count_where(lambda x: x == lo0)
        unequal = tied & (n_eq != t_cnt - n_hi0)

        def refined(args):
            def refine(_, carry):
                lo, hi, n_hi = carry
                mid = lo + (hi - lo) * 0.5
                cnt = count_ge(mid)
                ok = cnt >= kf
                return jnp.where(ok, mid, lo), jnp.where(ok, hi, mid), jnp.where(ok, n_hi, cnt)
            return lax.fori_loop(0, N_REFINE, refine, args)

        lo8, hi8, n_hi8 = lax.cond(jnp.max(jnp.where(unequal, 1.0, 0.0)) > 0.0, refined, lambda args: args,
                                   (lo0, hi0, n_hi0))
        need8 = kf - n_hi8
        t_sel8 = _key_to_float(jnp.maximum(t_key, KEY_NEG_INF + 1))
        slabs = (kc // SUBLANES, SUBLANES, tq)

        def tie_chunk(c, seen8):
            x = xs[c].reshape(slabs)
            above = jnp.where(x >= hi8, 1.0, 0.0)
            tie = jnp.where(x >= lo8, 1.0, 0.0) - above
            before = _dot(earlier[...], tie.reshape(kc, tq).astype(bf16)).reshape(slabs)
            sel = above + jnp.where(before + seen8 < need8, tie, 0.0)
            sel = jnp.where(real, sel, jnp.where(x >= t_sel8, 1.0, 0.0))
            bias[c] = ((sel - 1.0) * -NEG_BIG).reshape(kc, tq).astype(bias.dtype)
            total = jnp.sum(jnp.sum(tie, axis=0), axis=0, keepdims=True)
            return seen8 + jnp.broadcast_to(total, (SUBLANES, tq))

        lax.fori_loop(0, n_chunks, tie_chunk, jnp.zeros((SUBLANES, tq), f32))

    m_ref[...] = jnp.full(m_ref.shape, -jnp.inf, f32)
    acc_ref[...] = jnp.zeros(acc_ref.shape, f32)

    def attend_chunk(c, prod):
        b = bias[c]
        for h in range(DSA_HEADS):
            hs = slice(h * tq, (h + 1) * tq)
            s = prod[:, hs].astype(bf16) + b
            m_old = m_ref[:, hs]
            m_new = jnp.maximum(m_old, jnp.max(s, axis=0, keepdims=True).astype(f32))
            alpha_ref[:, hs] = jnp.exp2(m_old - m_new)
            p_ref[:, hs] = jnp.exp2(s - m_new.astype(bf16))
            m_ref[:, hs] = m_new
        acc_ref[...] = alpha_ref[...] * acc_ref[...] + _dot(vaug[c], p_ref[...])

    pipelined(kk_ref, lhs_a, attend_chunk)

    for p in range(N_PAIRS):
        halves = []
        for e in range(PAIR):
            hs = slice((PAIR * p + e) * tq, (PAIR * p + e + 1) * tq)
            halves.append(acc_ref[0:HEAD_DIM, hs] * (1.0 / acc_ref[HEAD_DIM:HEAD_DIM + 1, hs]))
        out_ref[:, p * LANES:(p + 1) * LANES] = jnp.concatenate(halves, axis=0).T.astype(out_ref.dtype)


def _dsa(aq, iq, iw, kk, vv, ik, batch, seq):
    t = aq.shape[0]
    tq = min(TQ, seq)
    kc = min(KC, seq)
    nq = seq // tq
    k_sel = min(TOPK_MAX, seq // 4)
    heads = DSA_HEADS
    qrow = lambda b, j: (b * nq + j, 0)
    brow = lambda b, j: (b, 0)
    return pl.pallas_call(
        functools.partial(_dsa_kernel, k_sel),
        grid=(batch, nq),
        in_specs=[pl.BlockSpec((tq, GROUP), qrow), pl.BlockSpec((tq, GROUP), qrow),
                  pl.BlockSpec((tq, LANES), qrow),
                  pl.BlockSpec((seq, LANES), brow), pl.BlockSpec((seq, LANES), brow),
                  pl.BlockSpec((seq, LANES), brow)],
        out_specs=pl.BlockSpec((tq, GROUP), qrow),
        out_shape=jax.ShapeDtypeStruct((t, GROUP), bf16),
        scratch_shapes=[
            pltpu.VMEM((heads * tq, LANES), bf16),
            pltpu.VMEM((heads * tq, LANES), bf16),
            pltpu.VMEM((seq // kc, kc, tq), f32),
            pltpu.VMEM((seq // kc, kc, tq), bf16),
            pltpu.VMEM((kc, heads * tq), f32),
            pltpu.VMEM((kc, heads * tq), f32),
            pltpu.VMEM((kc, heads * tq), bf16),
            pltpu.VMEM((1, heads * tq), f32),
            pltpu.VMEM((1, heads * tq), f32),
            pltpu.VMEM((HEAD_DIM + SUM_ROWS, heads * tq), f32),
            pltpu.VMEM((seq // kc, HEAD_DIM + SUM_ROWS, kc), bf16),
            pltpu.VMEM((kc, kc), bf16),
        ],
        compiler_params=pltpu.CompilerParams(dimension_semantics=("arbitrary", "arbitrary"),
                                             vmem_limit_bytes=VMEM_LIMIT),
        name="dsa",
    )(aq, iq, iw, kk, vv, ik)


def _outproj_kernel(ret_ref, att_ref, x_ref, w_ref, gpost_ref, gpre_ref, h_ref, hn_ref):
    mixed = _dot(ret_ref[...], w_ref[0:GROUP, :]) + _dot(att_ref[...], w_ref[GROUP:2 * GROUP, :])
    h = x_ref[...] + _rms(mixed, gpost_ref[...])
    h_ref[...] = h
    hn_ref[...] = _rms(h, gpre_ref[...]).astype(hn_ref.dtype)


def _outproj(ret, att, x2, w, g_post, g_pre, seq):
    t, d = x2.shape
    tm = min(TM_PROJ, seq)
    row = lambda i: (i, 0)
    fixed = lambda i: (0, 0)
    return pl.pallas_call(
        _outproj_kernel,
        grid=(t // tm,),
        in_specs=[pl.BlockSpec((tm, GROUP), row), pl.BlockSpec((tm, GROUP), row),
                  pl.BlockSpec((tm, d), row), pl.BlockSpec((2 * GROUP, d), fixed),
                  pl.BlockSpec((1, d), fixed), pl.BlockSpec((1, d), fixed)],
        out_specs=[pl.BlockSpec((tm, d), row), pl.BlockSpec((tm, d), row)],
        out_shape=[jax.ShapeDtypeStruct((t, d), f32), jax.ShapeDtypeStruct((t, d), bf16)],
        compiler_params=pltpu.CompilerParams(dimension_semantics=("arbitrary",),
                                             vmem_limit_bytes=VMEM_LIMIT),
        name="out_proj",
    )(ret, att, x2, w, g_post, g_pre)


def _ffn_kernel(hn_ref, h_ref, wg_ref, wu_ref, cwg_ref, cwu_ref, cbg_ref, cbu_ref, wd_ref, gpost_ref,
                out_ref, acc_ref, carry_ref):
    i = pl.program_id(1)
    f = pl.program_id(2)
    tm = hn_ref.shape[0]

    @pl.when(f == 0)
    def _():
        acc_ref[...] = jnp.zeros_like(acc_ref)

    @pl.when(i == 0)
    def _():
        carry_ref[f] = jnp.zeros(carry_ref.shape[1:], f32)

    hn = hn_ref[...]

    row = lax.broadcasted_iota(jnp.int32, (tm, wg_ref.shape[1]), 0)

    def conv(up, w_ref, b_ref, tail):
        m1 = jnp.where(row == 0, tail[7:8, :], pltpu.roll(up, 1, 0))
        m2 = jnp.where(row == 0, tail[6:7, :], jnp.where(row == 1, tail[7:8, :], pltpu.roll(up, 2, 0)))
        return b_ref[...] + m2 * w_ref[0:1, :] + m1 * w_ref[1:2, :] + up * w_ref[2:3, :]

    up_g = _dot(hn, wg_ref[...])
    up_u = _dot(hn, wu_ref[...])
    g = conv(up_g, cwg_ref, cbg_ref, carry_ref[f, 0])
    u = conv(up_u, cwu_ref, cbu_ref, carry_ref[f, 1])
    carry_ref[f, 0] = up_g[tm - 8:tm, :]
    carry_ref[f, 1] = up_u[tm - 8:tm, :]
    acc_ref[...] += _dot((_silu(g) * u).astype(bf16), wd_ref[...])

    @pl.when(f == pl.num_programs(2) - 1)
    def _():
        out_ref[...] = h_ref[...] + _rms(acc_ref[...], gpost_ref[...])


def _ffn(hn, h, w_up, conv_w, conv_b, w_down, g_post, batch, seq):
    t, d = h.shape
    tm = min(TM_FFN, seq)
    n_seq = seq // tm
    n_f = D_FF // TF
    row = lambda b, i, f: (b * n_seq + i, 0)
    return pl.pallas_call(
        _ffn_kernel,
        grid=(batch, n_seq, n_f),
        in_specs=[pl.BlockSpec((tm, d), row), pl.BlockSpec((tm, d), row),
                  pl.BlockSpec((d, TF), lambda b, i, f: (0, f)),
                  pl.BlockSpec((d, TF), lambda b, i, f: (0, n_f + f)),
                  pl.BlockSpec((CONV_WIDTH, TF), lambda b, i, f: (0, f)),
                  pl.BlockSpec((CONV_WIDTH, TF), lambda b, i, f: (0, n_f + f)),
                  pl.BlockSpec((1, TF), lambda b, i, f: (0, f)),
                  pl.BlockSpec((1, TF), lambda b, i, f: (0, n_f + f)),
                  pl.BlockSpec((TF, d), lambda b, i, f: (f, 0)),
                  pl.BlockSpec((1, d), lambda b, i, f: (0, 0))],
        out_specs=pl.BlockSpec((tm, d), row),
        out_shape=jax.ShapeDtypeStruct((t, d), f32),
        scratch_shapes=[pltpu.VMEM((tm, d), f32), pltpu.VMEM((n_f, 2, SUBLANES, TF), f32)],
        compiler_params=pltpu.CompilerParams(dimension_semantics=("arbitrary", "arbitrary", "arbitrary"),
                                             vmem_limit_bytes=VMEM_LIMIT),
        name="ffn",
    )(hn, h, w_up, w_up, conv_w, conv_w, conv_b, conv_b, w_down, g_post)


def _relayout_w_in(w):
    d = w.shape[0]
    sizes = (GROUP, GROUP, GROUP, GROUP, GROUP, HEAD_DIM, HEAD_DIM, IDX_HEADS * IDX_DIM, IDX_DIM, IDX_HEADS)
    parts, off = [], 0
    for n in sizes:
        parts.append(w[:, off:off + n])
        off += n
    rq, rk, rv, rg, aq, ak, av, iq, ik, iw = parts
    pad = jnp.zeros((d, LANES - IDX_HEADS), w.dtype)
    return jnp.concatenate([rq, rk, aq, iq, rv, rg, ak, ak, av, av, ik, ik, iw, pad], axis=1).astype(bf16)


def _rope_tables(seq):
    half = HEAD_DIM // 2
    inv = ROPE_THETA ** (-jnp.arange(0, HEAD_DIM, 2, dtype=f32) / HEAD_DIM)
    ang = jnp.arange(seq, dtype=f32)[:, None] * inv[None, :]
    cos, sin = jnp.cos(ang), jnp.sin(ang)
    cos_t = jnp.tile(cos, (1, LANES // half))
    sin_t = jnp.tile(jnp.concatenate([-sin, sin], axis=1), (1, PAIR))
    return cos_t, sin_t


def kernel(x, mix_norm_pre, mix_norm_post, w_in, w_out, ffn_norm_pre, ffn_norm_post,
           w_up, conv_w, conv_b, w_down):
    batch, seq, d = x.shape
    depth = w_in.shape[0]
    cos_t, sin_t = _rope_tables(seq)
    h = x.reshape(batch * seq, d)
    for l in range(depth):
        rq, rk, aq, iq, rv, rg, kk, vv, ik, iw = _inproj(
            h, mix_norm_pre[l][None, :], _relayout_w_in(w_in[l]), cos_t, sin_t, seq)
        ret = _retention(rq, rk, rv, rg, batch, seq)
        att = _dsa(aq, iq, iw, kk, vv, ik, batch, seq)
        h, hn = _outproj(ret, att, h, w_out[l].astype(bf16), mix_norm_post[l][None, :],
                         ffn_norm_pre[l][None, :], seq)
        h = _ffn(hn, h, w_up[l].astype(bf16), conv_w[l], conv_b[l][None, :], w_down[l].astype(bf16),
                 ffn_norm_post[l][None, :], batch, seq)
    return h.reshape(batch, seq, d)
```

```python
import functools

import jax
import jax.numpy as jnp
from jax import lax
from jax.experimental import pallas as pl
from jax.experimental.pallas import tpu as pltpu

HEAD_DIM = 64
RET_HEADS = 8
DSA_HEADS = 8
IDX_HEADS = 8
IDX_DIM = 64
TOPK_MAX = 256
D_FF = 2816
CONV_WIDTH = 3
ROPE_THETA = 10000.0
EPS = 1e-6

LANES = 128
SUBLANES = 8
PAIR = LANES // HEAD_DIM
N_PAIRS = RET_HEADS // PAIR
GROUP = RET_HEADS * HEAD_DIM

_OFF_RQ, _OFF_RK, _OFF_AQ, _OFF_IQ, _OFF_RV, _OFF_RG = (i * GROUP for i in range(6))
_OFF_KK = 6 * GROUP
_OFF_VV = _OFF_KK + LANES
_OFF_IK = _OFF_VV + LANES
_OFF_IW = _OFF_IK + LANES
_W_TOTAL = _OFF_IW + LANES

TM_PROJ = 512
RET_CHUNK = 256
TQ = 256
KC = 512
TM_FFN = 1024
TF = 1408
VMEM_LIMIT = 48 * 1024 * 1024

NEG_BIG = -1e30
INT_MIN = -2 ** 31
KEY_NEG_INF = -2139095041
KEY16_NEG_INF = -32641
PACKED_ROWS = 16
N_COUNT_ACC = 4
N_REFINE = 12
SUM_ROWS = 16
LOG2E = 1.4426950408889634

f32 = jnp.float32
bf16 = jnp.bfloat16


def _dot(a, b):
    return jnp.dot(a, b, preferred_element_type=f32)


def _dot_nt(a, b):
    return lax.dot_general(a, b, (((1,), (1,)), ((), ())), preferred_element_type=f32)


def _dot_tn(a, b):
    return lax.dot_general(a, b, (((0,), (0,)), ((), ())), preferred_element_type=f32)


def _rms(x, g):
    return x * lax.rsqrt(jnp.mean(x * x, axis=-1, keepdims=True) + EPS) * g


def _silu(x):
    return x * (1.0 / (1.0 + jnp.exp(-x)))


def _inproj_kernel(x_ref, g_ref, w_ref, ch_ref, sh_ref,
                   rq_ref, rk_ref, aq_ref, iq_ref, rv_ref, rg_ref, kk_ref, vv_ref, ik_ref, iw_ref):
    tm = x_ref.shape[0]
    hn = _rms(x_ref[...], g_ref[...]).astype(bf16)
    cos = ch_ref[...]
    sin = sh_ref[...]
    lane = lax.broadcasted_iota(jnp.int32, (tm, LANES), 1)
    first_half = (lane % HEAD_DIM) < (HEAD_DIM // 2)

    def proj(off, width):
        return _dot(hn, w_ref[:, off:off + width])

    def rope(p):
        swapped = jnp.where(first_half, pltpu.roll(p, LANES - HEAD_DIM // 2, 1),
                            pltpu.roll(p, HEAD_DIM // 2, 1))
        return p * cos + swapped * sin

    def roped_group(off, out_ref, scale):
        p = proj(off, GROUP)
        for b in range(GROUP // LANES):
            r = rope(p[:, b * LANES:(b + 1) * LANES])
            if scale != 1.0:
                r = r * scale
            out_ref[:, b * LANES:(b + 1) * LANES] = r.astype(out_ref.dtype)

    roped_group(_OFF_RQ, rq_ref, 1.0)
    roped_group(_OFF_RK, rk_ref, HEAD_DIM ** -0.5)
    roped_group(_OFF_AQ, aq_ref, HEAD_DIM ** -0.5 * LOG2E)
    roped_group(_OFF_IQ, iq_ref, 1.0)
    rv_ref[...] = proj(_OFF_RV, GROUP).astype(bf16)
    rg_ref[...] = proj(_OFF_RG, GROUP)
    small = proj(_OFF_KK, 4 * LANES)
    kk_ref[...] = rope(small[:, 0:LANES]).astype(bf16)
    vv_ref[...] = small[:, LANES:2 * LANES].astype(bf16)
    ik_ref[...] = rope(small[:, 2 * LANES:3 * LANES]).astype(bf16)
    iw_ref[...] = small[:, 3 * LANES:4 * LANES] * (IDX_HEADS ** -0.5 * IDX_DIM ** -0.5)


def _inproj(x2, g, w, cos_t, sin_t, seq):
    t, d = x2.shape
    tm = min(TM_PROJ, seq)
    n_seq = seq // tm
    row = lambda i: (i, 0)
    fixed = lambda i: (0, 0)
    pos = lambda i: (i % n_seq, 0)
    wide = lambda dt: jax.ShapeDtypeStruct((t, GROUP), dt)
    narrow = lambda dt: jax.ShapeDtypeStruct((t, LANES), dt)
    return pl.pallas_call(
        _inproj_kernel,
        grid=(t // tm,),
        in_specs=[pl.BlockSpec((tm, d), row), pl.BlockSpec((1, d), fixed),
                  pl.BlockSpec((d, _W_TOTAL), fixed),
                  pl.BlockSpec((tm, LANES), pos), pl.BlockSpec((tm, LANES), pos)],
        out_specs=[pl.BlockSpec((tm, GROUP), row)] * 6 + [pl.BlockSpec((tm, LANES), row)] * 4,
        out_shape=[wide(bf16), wide(bf16), wide(bf16), wide(bf16), wide(bf16), wide(f32),
                   narrow(bf16), narrow(bf16), narrow(bf16), narrow(f32)],
        compiler_params=pltpu.CompilerParams(dimension_semantics=("arbitrary",),
                                             vmem_limit_bytes=VMEM_LIMIT),
        name="in_proj",
    )(x2, g, w, cos_t, sin_t)


def _retention_kernel(rq_ref, rk_ref, rv_ref, rg_ref, decay_ref, zeta_ref, xi_ref, cd_ref,
                      out_ref, state_ref):
    c = rq_ref.shape[0]

    @pl.when(pl.program_id(1) == 0)
    def _():
        state_ref[...] = jnp.zeros_like(state_ref)

    lane = lax.broadcasted_iota(jnp.int32, (c, LANES), 1)
    even = lane < HEAD_DIM
    r_i = lax.broadcasted_iota(jnp.int32, (LANES, LANES), 0)
    c_i = lax.broadcasted_iota(jnp.int32, (LANES, LANES), 1)
    same_head = (r_i < HEAD_DIM) == (c_i < HEAD_DIM)
    ones_bd = jnp.where(same_head, 1.0, 0.0).astype(bf16)

    for p in range(N_PAIRS):
        sl = slice(p * LANES, (p + 1) * LANES)
        q2 = rq_ref[:, sl]
        k2 = rk_ref[:, sl]
        v2 = rv_ref[:, sl]
        zero = jnp.zeros_like(q2)
        s_e = _dot_nt(jnp.where(even, q2, zero), k2) * decay_ref[2 * p]
        s_o = _dot_nt(jnp.where(even, zero, q2), k2) * decay_ref[2 * p + 1]
        inner = jnp.where(even, _dot(s_e.astype(bf16), v2), _dot(s_o.astype(bf16), v2))
        r_prev = state_ref[p]
        cross = _dot(q2, r_prev.astype(bf16)) * xi_ref[p]
        o = inner + cross
        kz = (k2.astype(f32) * zeta_ref[p]).astype(bf16)
        s_new = jnp.where(same_head, _dot_tn(kz, v2), 0.0)
        state_ref[p] = cd_ref[p] * r_prev + s_new
        sq = o * o
        hi = sq.astype(bf16)
        lo = (sq - hi.astype(f32)).astype(bf16)
        ms = (_dot(hi, ones_bd) + _dot(lo, ones_bd)) * (1.0 / HEAD_DIM)
        o = o * lax.rsqrt(ms + EPS)
        out_ref[:, sl] = (o * _silu(rg_ref[:, sl])).astype(out_ref.dtype)


def _retention_tables(c):
    h = jnp.arange(RET_HEADS, dtype=f32)
    log_g = jnp.log(1.0 - 2.0 ** (-5.0 - h))
    pos = jnp.arange(c, dtype=f32)
    diff = pos[:, None] - pos[None, :]
    decay = jnp.where(diff >= 0, jnp.exp(log_g[:, None, None] * jnp.maximum(diff, 0.0)), 0.0)
    zeta = jnp.exp(log_g[:, None] * (c - 1.0 - pos)[None, :])
    xi = jnp.exp(log_g[:, None] * (pos + 1.0)[None, :])
    cd = jnp.exp(log_g * c)

    def pair_lanes(a):
        a = a.reshape(N_PAIRS, PAIR, c)
        return jnp.repeat(jnp.transpose(a, (0, 2, 1)), HEAD_DIM, axis=2)

    cd2 = jnp.repeat(cd.reshape(N_PAIRS, 1, PAIR), HEAD_DIM, axis=2)
    return decay, pair_lanes(zeta), pair_lanes(xi), cd2


def _retention(rq, rk, rv, rg, batch, seq):
    t = rq.shape[0]
    c = min(RET_CHUNK, seq)
    n = seq // c
    decay, zeta, xi, cd = _retention_tables(c)
    row = lambda b, i: (b * n + i, 0)
    fix3 = lambda b, i: (0, 0, 0)
    return pl.pallas_call(
        _retention_kernel,
        grid=(batch, n),
        in_specs=[pl.BlockSpec((c, GROUP), row)] * 4 + [
            pl.BlockSpec((RET_HEADS, c, c), fix3), pl.BlockSpec((N_PAIRS, c, LANES), fix3),
            pl.BlockSpec((N_PAIRS, c, LANES), fix3), pl.BlockSpec((N_PAIRS, 1, LANES), fix3)],
        out_specs=pl.BlockSpec((c, GROUP), row),
        out_shape=jax.ShapeDtypeStruct((t, GROUP), bf16),
        scratch_shapes=[pltpu.VMEM((N_PAIRS, LANES, LANES), f32)],
        compiler_params=pltpu.CompilerParams(dimension_semantics=("arbitrary", "arbitrary"),
                                             vmem_limit_bytes=VMEM_LIMIT),
        name="retention",
    )(rq, rk, rv, rg, decay, zeta, xi, cd)


def _key_to_float(key):
    bits = jnp.where(key >= 0, key, key ^ jnp.int32(0x7FFFFFFF))
    return lax.bitcast_convert_type(bits, f32)


def _dsa_kernel(k_sel, aq_ref, iq_ref, iw_ref, kk_ref, vv_ref, ik_ref, out_ref,
                lhs_a, lhs_i, xs, xb, bias, prod, p_ref, m_ref, alpha_ref, acc_ref, vaug, earlier):
    tq = aq_ref.shape[0]
    n_kc, kc, _ = xs.shape
    j = pl.program_id(1)
    n_chunks = ((j + 1) * tq + kc - 1) // kc
    kf = float(k_sel)

    @pl.when((pl.program_id(0) == 0) & (j == 0))
    def _():
        r_i = lax.broadcasted_iota(jnp.int32, (kc, kc), 0)
        c_i = lax.broadcasted_iota(jnp.int32, (kc, kc), 1)
        earlier[...] = jnp.where(c_i < r_i, 1.0, 0.0).astype(bf16)

    @pl.when(j == 0)
    def _():
        row = lax.broadcasted_iota(jnp.int32, (LANES, kc), 0)
        for c in range(n_kc):
            vt = vv_ref[c * kc:(c + 1) * kc, :].astype(f32).T
            vaug[c] = jnp.where(row < HEAD_DIM, vt, 1.0)[0:HEAD_DIM + SUM_ROWS, :].astype(bf16)

    lane = lax.broadcasted_iota(jnp.int32, (tq, LANES), 1)
    even = lane < HEAD_DIM
    for p in range(N_PAIRS):
        sl = slice(p * LANES, (p + 1) * LANES)
        a2 = aq_ref[:, sl]
        i2 = iq_ref[:, sl]
        zero = jnp.zeros_like(a2)
        lhs_a[(2 * p) * tq:(2 * p + 1) * tq, :] = jnp.where(even, a2, zero)
        lhs_a[(2 * p + 1) * tq:(2 * p + 2) * tq, :] = jnp.where(even, zero, a2)
        lhs_i[(2 * p) * tq:(2 * p + 1) * tq, :] = jnp.where(even, i2, zero)
        lhs_i[(2 * p + 1) * tq:(2 * p + 2) * tq, :] = jnp.where(even, zero, i2)
    w_heads = iw_ref[...].T[0:IDX_HEADS, :]

    key_row = lax.broadcasted_iota(jnp.int32, (kc, tq), 0)
    q_pos = j * tq + lax.broadcasted_iota(jnp.int32, (kc, tq), 1)

    def for_key_chunks(key_ref, lhs_ref, work):
        def one_chunk(c, _):
            start = pl.multiple_of(c * kc, kc)
            prod[...] = _dot_nt(key_ref[pl.ds(start, kc), :], lhs_ref[...])
            work(c, prod)
            return 0

        lax.fori_loop(0, n_chunks, one_chunk, 0)

    def score_chunk(c, prod):
        isc = jnp.zeros((kc, tq), f32)
        for h in range(IDX_HEADS):
            isc = isc + w_heads[h:h + 1, :] * jnp.maximum(prod[:, h * tq:(h + 1) * tq], 0.0)
        isc = jnp.where(c * kc + key_row <= q_pos, isc, -jnp.inf)
        xs[c] = isc
        xb[c] = isc.astype(xb.dtype)

    for_key_chunks(ik_ref, lhs_i, score_chunk)

    def count_where(pred):
        def body(c, accs):
            x = xs[c]
            accs = list(accs)
            for r in range(kc // SUBLANES):
                hit = jnp.where(pred(x[r * SUBLANES:(r + 1) * SUBLANES, :]), 1.0, 0.0)
                accs[r % N_COUNT_ACC] = accs[r % N_COUNT_ACC] + hit
            return tuple(accs)
        zero = jnp.zeros((SUBLANES, tq), f32)
        accs = lax.fori_loop(0, n_chunks, body, (zero,) * N_COUNT_ACC)
        total = functools.reduce(lambda a, b: a + b, accs)
        return jnp.broadcast_to(jnp.sum(total, axis=0, keepdims=True), (SUBLANES, tq))

    def count_ge(cand_f):
        return count_where(lambda x: x >= cand_f)

    def count_ge_packed(cand_b):
        one = jnp.ones((PACKED_ROWS, tq), xb.dtype)
        zero = jnp.zeros((PACKED_ROWS, tq), xb.dtype)

        def body(c, accs):
            x = xb[c]
            accs = list(accs)
            for r in range(kc // PACKED_ROWS):
                hit = jnp.where(x[r * PACKED_ROWS:(r + 1) * PACKED_ROWS, :] >= cand_b, one, zero)
                accs[r % N_COUNT_ACC] = accs[r % N_COUNT_ACC] + hit
            return tuple(accs)
        accs = lax.fori_loop(0, n_chunks, body, (zero,) * N_COUNT_ACC)
        total = functools.reduce(lambda a, b: a + b, [a.astype(f32) for a in accs])
        return jnp.broadcast_to(jnp.sum(total, axis=0, keepdims=True), (SUBLANES, tq))

    def key16_to_f32_bits(key16):
        b16 = jnp.where(key16 >= 0, key16, key16 ^ jnp.int32(0x7FFF)) & jnp.int32(0xFFFF)
        return lax.shift_left(b16, jnp.int32(16))

    def bit_step_packed(i, t16):
        cand = t16 + lax.shift_left(jnp.int32(1), jnp.int32(15) - i)
        cand_f = lax.bitcast_convert_type(key16_to_f32_bits(cand), f32)
        cand_b = jnp.broadcast_to(cand_f[0:1, :].astype(xb.dtype), (PACKED_ROWS, tq))
        return jnp.where(count_ge_packed(cand_b) >= kf, cand, t16)

    t16 = lax.fori_loop(0, 16, bit_step_packed, jnp.full((SUBLANES, tq), -2 ** 15, jnp.int32))

    c0_bits = key16_to_f32_bits(t16)
    c0_key = jnp.where(c0_bits >= 0, c0_bits, c0_bits ^ jnp.int32(0x7FFFFFFF))
    has_k = t16 > KEY16_NEG_INF

    def bit_step(i, carry):
        t_key, t_cnt = carry
        cand = t_key + lax.shift_left(jnp.int32(1), jnp.int32(16) - i)
        cnt = count_ge(_key_to_float(cand))
        ok = cnt >= kf
        return jnp.where(ok, cand, t_key), jnp.where(ok, cnt, t_cnt)

    t_key, t_cnt = lax.fori_loop(0, 17, bit_step, (c0_key - 2 ** 16, jnp.full((SUBLANES, tq), kf, f32)))
    t_key = jnp.where(has_k, t_key, INT_MIN)

    t_sel = _key_to_float(jnp.maximum(t_key, KEY_NEG_INF + 1))[0:1, :]

    def bias_chunk(c, _):
        bias[c] = jnp.where(xs[c] >= t_sel, 0.0, NEG_BIG).astype(bias.dtype)
        return 0

    lax.fori_loop(0, n_chunks, bias_chunk, 0)

    real = t_key > KEY_NEG_INF
    tied = (t_cnt > kf) & real

    @pl.when(jnp.max(jnp.where(tied, 1.0, 0.0)) > 0.0)
    def _():
        lo0 = _key_to_float(t_key)
        hi0 = _key_to_float(t_key + 1)
        n_hi0 = count_ge(hi0)
        n_eq = count_where(lambda x: x == lo0)
        unequal = tied & (n_eq != t_cnt - n_hi0)

        def refined(args):
            def refine(_, carry):
                lo, hi, n_hi = carry
                mid = lo + (hi - lo) * 0.5
                cnt = count_ge(mid)
                ok = cnt >= kf
                return jnp.where(ok, mid, lo), jnp.where(ok, hi, mid), jnp.where(ok, n_hi, cnt)
            return lax.fori_loop(0, N_REFINE, refine, args)

        lo8, hi8, n_hi8 = lax.cond(jnp.max(jnp.where(unequal, 1.0, 0.0)) > 0.0, refined, lambda args: args,
                                   (lo0, hi0, n_hi0))
        need8 = kf - n_hi8
        t_sel8 = _key_to_float(jnp.maximum(t_key, KEY_NEG_INF + 1))
        slabs = (kc // SUBLANES, SUBLANES, tq)

        def tie_chunk(c, seen8):
            x = xs[c].reshape(slabs)
            above = jnp.where(x >= hi8, 1.0, 0.0)
            tie = jnp.where(x >= lo8, 1.0, 0.0) - above
            before = _dot(earlier[...], tie.reshape(kc, tq).astype(bf16)).reshape(slabs)
            sel = above + jnp.where(before + seen8 < need8, tie, 0.0)
            sel = jnp.where(real, sel, jnp.where(x >= t_sel8, 1.0, 0.0))
            bias[c] = ((sel - 1.0) * -NEG_BIG).reshape(kc, tq).astype(bias.dtype)
            total = jnp.sum(jnp.sum(tie, axis=0), axis=0, keepdims=True)
            return seen8 + jnp.broadcast_to(total, (SUBLANES, tq))

        lax.fori_loop(0, n_chunks, tie_chunk, jnp.zeros((SUBLANES, tq), f32))

    m_ref[...] = jnp.full(m_ref.shape, -jnp.inf, f32)
    acc_ref[...] = jnp.zeros(acc_ref.shape, f32)

    def attend_chunk(c, prod):
        b = bias[c]
        for h in range(DSA_HEADS):
            hs = slice(h * tq, (h + 1) * tq)
            s = prod[:, hs].astype(bf16) + b
            m_old = m_ref[:, hs]
            m_new = jnp.maximum(m_old, jnp.max(s, axis=0, keepdims=True).astype(f32))
            alpha_ref[:, hs] = jnp.exp2(m_old - m_new)
            p_ref[:, hs] = jnp.exp2(s - m_new.astype(bf16))
            m_ref[:, hs] = m_new
        acc_ref[...] = alpha_ref[...] * acc_ref[...] + _dot(vaug[c], p_ref[...])

    for_key_chunks(kk_ref, lhs_a, attend_chunk)

    for p in range(N_PAIRS):
        halves = []
        for e in range(PAIR):
            hs = slice((PAIR * p + e) * tq, (PAIR * p + e + 1) * tq)
            halves.append(acc_ref[0:HEAD_DIM, hs] * (1.0 / acc_ref[HEAD_DIM:HEAD_DIM + 1, hs]))
        out_ref[:, p * LANES:(p + 1) * LANES] = jnp.concatenate(halves, axis=0).T.astype(out_ref.dtype)


def _dsa(aq, iq, iw, kk, vv, ik, batch, seq):
    t = aq.shape[0]
    tq = min(TQ, seq)
    kc = min(KC, seq)
    nq = seq // tq
    k_sel = min(TOPK_MAX, seq // 4)
    heads = DSA_HEADS
    qrow = lambda b, j: (b * nq + j, 0)
    brow = lambda b, j: (b, 0)
    return pl.pallas_call(
        functools.partial(_dsa_kernel, k_sel),
        grid=(batch, nq),
        in_specs=[pl.BlockSpec((tq, GROUP), qrow), pl.BlockSpec((tq, GROUP), qrow),
                  pl.BlockSpec((tq, LANES), qrow),
                  pl.BlockSpec((seq, LANES), brow), pl.BlockSpec((seq, LANES), brow),
                  pl.BlockSpec((seq, LANES), brow)],
        out_specs=pl.BlockSpec((tq, GROUP), qrow),
        out_shape=jax.ShapeDtypeStruct((t, GROUP), bf16),
        scratch_shapes=[
            pltpu.VMEM((heads * tq, LANES), bf16),
            pltpu.VMEM((heads * tq, LANES), bf16),
            pltpu.VMEM((seq // kc, kc, tq), f32),
            pltpu.VMEM((seq // kc, kc, tq), bf16),
            pltpu.VMEM((seq // kc, kc, tq), bf16),
            pltpu.VMEM((kc, heads * tq), f32),
            pltpu.VMEM((kc, heads * tq), bf16),
            pltpu.VMEM((1, heads * tq), f32),
            pltpu.VMEM((1, heads * tq), f32),
            pltpu.VMEM((HEAD_DIM + SUM_ROWS, heads * tq), f32),
            pltpu.VMEM((seq // kc, HEAD_DIM + SUM_ROWS, kc), bf16),
            pltpu.VMEM((kc, kc), bf16),
        ],
        compiler_params=pltpu.CompilerParams(dimension_semantics=("arbitrary", "arbitrary"),
                                             vmem_limit_bytes=VMEM_LIMIT),
        name="dsa",
    )(aq, iq, iw, kk, vv, ik)


def _outproj_kernel(ret_ref, att_ref, x_ref, w_ref, gpost_ref, gpre_ref, h_ref, hn_ref):
    mixed = _dot(ret_ref[...], w_ref[0:GROUP, :]) + _dot(att_ref[...], w_ref[GROUP:2 * GROUP, :])
    h = x_ref[...] + _rms(mixed, gpost_ref[...])
    h_ref[...] = h
    hn_ref[...] = _rms(h, gpre_ref[...]).astype(hn_ref.dtype)


def _outproj(ret, att, x2, w, g_post, g_pre, seq):
    t, d = x2.shape
    tm = min(TM_PROJ, seq)
    row = lambda i: (i, 0)
    fixed = lambda i: (0, 0)
    return pl.pallas_call(
        _outproj_kernel,
        grid=(t // tm,),
        in_specs=[pl.BlockSpec((tm, GROUP), row), pl.BlockSpec((tm, GROUP), row),
                  pl.BlockSpec((tm, d), row), pl.BlockSpec((2 * GROUP, d), fixed),
                  pl.BlockSpec((1, d), fixed), pl.BlockSpec((1, d), fixed)],
        out_specs=[pl.BlockSpec((tm, d), row), pl.BlockSpec((tm, d), row)],
        out_shape=[jax.ShapeDtypeStruct((t, d), f32), jax.ShapeDtypeStruct((t, d), bf16)],
        compiler_params=pltpu.CompilerParams(dimension_semantics=("arbitrary",),
                                             vmem_limit_bytes=VMEM_LIMIT),
        name="out_proj",
    )(ret, att, x2, w, g_post, g_pre)


def _ffn_kernel(hn_ref, h_ref, wg_ref, wu_ref, cwg_ref, cwu_ref, cbg_ref, cbu_ref, wd_ref, gpost_ref,
                out_ref, acc_ref, carry_ref):
    i = pl.program_id(1)
    f = pl.program_id(2)
    tm = hn_ref.shape[0]

    @pl.when(f == 0)
    def _():
        acc_ref[...] = jnp.zeros_like(acc_ref)

    @pl.when(i == 0)
    def _():
        carry_ref[f] = jnp.zeros(carry_ref.shape[1:], f32)

    hn = hn_ref[...]

    row = lax.broadcasted_iota(jnp.int32, (tm, wg_ref.shape[1]), 0)

    def conv(up, w_ref, b_ref, tail):
        m1 = jnp.where(row == 0, tail[7:8, :], pltpu.roll(up, 1, 0))
        m2 = jnp.where(row == 0, tail[6:7, :], jnp.where(row == 1, tail[7:8, :], pltpu.roll(up, 2, 0)))
        return b_ref[...] + m2 * w_ref[0:1, :] + m1 * w_ref[1:2, :] + up * w_ref[2:3, :]

    up_g = _dot(hn, wg_ref[...])
    up_u = _dot(hn, wu_ref[...])
    g = conv(up_g, cwg_ref, cbg_ref, carry_ref[f, 0])
    u = conv(up_u, cwu_ref, cbu_ref, carry_ref[f, 1])
    carry_ref[f, 0] = up_g[tm - 8:tm, :]
    carry_ref[f, 1] = up_u[tm - 8:tm, :]
    acc_ref[...] += _dot((_silu(g) * u).astype(bf16), wd_ref[...])

    @pl.when(f == pl.num_programs(2) - 1)
    def _():
        out_ref[...] = h_ref[...] + _rms(acc_ref[...], gpost_ref[...])


def _ffn(hn, h, w_up, conv_w, conv_b, w_down, g_post, batch, seq):
    t, d = h.shape
    tm = min(TM_FFN, seq)
    n_seq = seq // tm
    n_f = D_FF // TF
    row = lambda b, i, f: (b * n_seq + i, 0)
    return pl.pallas_call(
        _ffn_kernel,
        grid=(batch, n_seq, n_f),
        in_specs=[pl.BlockSpec((tm, d), row), pl.BlockSpec((tm, d), row),
                  pl.BlockSpec((d, TF), lambda b, i, f: (0, f)),
                  pl.BlockSpec((d, TF), lambda b, i, f: (0, n_f + f)),
                  pl.BlockSpec((CONV_WIDTH, TF), lambda b, i, f: (0, f)),
                  pl.BlockSpec((CONV_WIDTH, TF), lambda b, i, f: (0, n_f + f)),
                  pl.BlockSpec((1, TF), lambda b, i, f: (0, f)),
                  pl.BlockSpec((1, TF), lambda b, i, f: (0, n_f + f)),
                  pl.BlockSpec((TF, d), lambda b, i, f: (f, 0)),
                  pl.BlockSpec((1, d), lambda b, i, f: (0, 0))],
        out_specs=pl.BlockSpec((tm, d), row),
        out_shape=jax.ShapeDtypeStruct((t, d), f32),
        scratch_shapes=[pltpu.VMEM((tm, d), f32), pltpu.VMEM((n_f, 2, SUBLANES, TF), f32)],
        compiler_params=pltpu.CompilerParams(dimension_semantics=("arbitrary", "arbitrary", "arbitrary"),
                                             vmem_limit_bytes=VMEM_LIMIT),
        name="ffn",
    )(hn, h, w_up, w_up, conv_w, conv_w, conv_b, conv_b, w_down, g_post)


def _relayout_w_in(w):
    d = w.shape[0]
    sizes = (GROUP, GROUP, GROUP, GROUP, GROUP, HEAD_DIM, HEAD_DIM, IDX_HEADS * IDX_DIM, IDX_DIM, IDX_HEADS)
    parts, off = [], 0
    for n in sizes:
        parts.append(w[:, off:off + n])
        off += n
    rq, rk, rv, rg, aq, ak, av, iq, ik, iw = parts
    pad = jnp.zeros((d, LANES - IDX_HEADS), w.dtype)
    return jnp.concatenate([rq, rk, aq, iq, rv, rg, ak, ak, av, av, ik, ik, iw, pad], axis=1).astype(bf16)


def _rope_tables(seq):
    half = HEAD_DIM // 2
    inv = ROPE_THETA ** (-jnp.arange(0, HEAD_DIM, 2, dtype=f32) / HEAD_DIM)
    ang = jnp.arange(seq, dtype=f32)[:, None] * inv[None, :]
    cos, sin = jnp.cos(ang), jnp.sin(ang)
    cos_t = jnp.tile(cos, (1, LANES // half))
    sin_t = jnp.tile(jnp.concatenate([-sin, sin], axis=1), (1, PAIR))
    return cos_t, sin_t


def kernel(x, mix_norm_pre, mix_norm_post, w_in, w_out, ffn_norm_pre, ffn_norm_post,
           w_up, conv_w, conv_b, w_down):
    batch, seq, d = x.shape
    depth = w_in.shape[0]
    cos_t, sin_t = _rope_tables(seq)
    h = x.reshape(batch * seq, d)
    for l in range(depth):
        rq, rk, aq, iq, rv, rg, kk, vv, ik, iw = _inproj(
            h, mix_norm_pre[l][None, :], _relayout_w_in(w_in[l]), cos_t, sin_t, seq)
        ret = _retention(rq, rk, rv, rg, batch, seq)
        att = _dsa(aq, iq, iw, kk, vv, ik, batch, seq)
        h, hn = _outproj(ret, att, h, w_out[l].astype(bf16), mix_norm_post[l][None, :],
                         ffn_norm_pre[l][None, :], seq)
        h = _ffn(hn, h, w_up[l].astype(bf16), conv_w[l], conv_b[l][None, :], w_down[l].astype(bf16),
                 ffn_norm_post[l][None, :], batch, seq)
    return h.reshape(batch, seq, d)
```

```python
import functools

import jax
import jax.numpy as jnp
from jax import lax
from jax.experimental import pallas as pl
from jax.experimental.pallas import tpu as pltpu

HEAD_DIM = 64
RET_HEADS = 8
DSA_HEADS = 8
IDX_HEADS = 8
IDX_DIM = 64
TOPK_MAX = 256
D_FF = 2816
CONV_WIDTH = 3
ROPE_THETA = 10000.0
EPS = 1e-6

LANES = 128
SUBLANES = 8
PAIR = LANES // HEAD_DIM
N_PAIRS = RET_HEADS // PAIR
GROUP = RET_HEADS * HEAD_DIM

_OFF_RQ, _OFF_RK, _OFF_AQ, _OFF_IQ, _OFF_RV, _OFF_RG = (i * GROUP for i in range(6))
_OFF_KK = 6 * GROUP
_OFF_VV = _OFF_KK + LANES
_OFF_IK = _OFF_VV + LANES
_OFF_IW = _OFF_IK + LANES
_W_TOTAL = _OFF_IW + LANES

TM_PROJ = 512
RET_CHUNK = 256
TQ = 256
KC = 512
TM_FFN = 512
TF = D_FF
VMEM_LIMIT = 48 * 1024 * 1024

NEG_BIG = -1e30
INT_MIN = -2 ** 31
KEY_NEG_INF = -2139095041
KEY16_NEG_INF = -32641
PACKED_ROWS = 16
N_COUNT_ACC = 4
N_REFINE = 12
SUM_ROWS = 16
LOG2E = 1.4426950408889634

f32 = jnp.float32
bf16 = jnp.bfloat16


def _dot(a, b):
    return jnp.dot(a, b, preferred_element_type=f32)


def _dot_nt(a, b):
    return lax.dot_general(a, b, (((1,), (1,)), ((), ())), preferred_element_type=f32)


def _dot_tn(a, b):
    return lax.dot_general(a, b, (((0,), (0,)), ((), ())), preferred_element_type=f32)


def _rms(x, g):
    return x * lax.rsqrt(jnp.mean(x * x, axis=-1, keepdims=True) + EPS) * g


def _silu(x):
    return x * (1.0 / (1.0 + jnp.exp(-x)))


def _inproj_kernel(x_ref, g_ref, w_ref, ch_ref, sh_ref,
                   rq_ref, rk_ref, aq_ref, iq_ref, rv_ref, rg_ref, kk_ref, vv_ref, ik_ref, iw_ref):
    tm = x_ref.shape[0]
    hn = _rms(x_ref[...], g_ref[...]).astype(bf16)
    cos = ch_ref[...]
    sin = sh_ref[...]
    lane = lax.broadcasted_iota(jnp.int32, (tm, LANES), 1)
    first_half = (lane % HEAD_DIM) < (HEAD_DIM // 2)

    def proj(off, width):
        return _dot(hn, w_ref[:, off:off + width])

    def rope(p):
        swapped = jnp.where(first_half, pltpu.roll(p, LANES - HEAD_DIM // 2, 1),
                            pltpu.roll(p, HEAD_DIM // 2, 1))
        return p * cos + swapped * sin

    def roped_group(off, out_ref, scale):
        p = proj(off, GROUP)
        for b in range(GROUP // LANES):
            r = rope(p[:, b * LANES:(b + 1) * LANES])
            if scale != 1.0:
                r = r * scale
            out_ref[:, b * LANES:(b + 1) * LANES] = r.astype(out_ref.dtype)

    roped_group(_OFF_RQ, rq_ref, 1.0)
    roped_group(_OFF_RK, rk_ref, HEAD_DIM ** -0.5)
    roped_group(_OFF_AQ, aq_ref, HEAD_DIM ** -0.5 * LOG2E)
    roped_group(_OFF_IQ, iq_ref, 1.0)
    rv_ref[...] = proj(_OFF_RV, GROUP).astype(bf16)
    rg_ref[...] = proj(_OFF_RG, GROUP)
    small = proj(_OFF_KK, 4 * LANES)
    kk_ref[...] = rope(small[:, 0:LANES]).astype(bf16)
    vv_ref[...] = small[:, LANES:2 * LANES].astype(bf16)
    ik_ref[...] = rope(small[:, 2 * LANES:3 * LANES]).astype(bf16)
    iw_ref[...] = small[:, 3 * LANES:4 * LANES] * (IDX_HEADS ** -0.5 * IDX_DIM ** -0.5)


def _inproj(x2, g, w, cos_t, sin_t, seq):
    t, d = x2.shape
    tm = min(TM_PROJ, seq)
    n_seq = seq // tm
    row = lambda i: (i, 0)
    fixed = lambda i: (0, 0)
    pos = lambda i: (i % n_seq, 0)
    wide = lambda dt: jax.ShapeDtypeStruct((t, GROUP), dt)
    narrow = lambda dt: jax.ShapeDtypeStruct((t, LANES), dt)
    return pl.pallas_call(
        _inproj_kernel,
        grid=(t // tm,),
        in_specs=[pl.BlockSpec((tm, d), row), pl.BlockSpec((1, d), fixed),
                  pl.BlockSpec((d, _W_TOTAL), fixed),
                  pl.BlockSpec((tm, LANES), pos), pl.BlockSpec((tm, LANES), pos)],
        out_specs=[pl.BlockSpec((tm, GROUP), row)] * 6 + [pl.BlockSpec((tm, LANES), row)] * 4,
        out_shape=[wide(bf16), wide(bf16), wide(bf16), wide(bf16), wide(bf16), wide(f32),
                   narrow(bf16), narrow(bf16), narrow(bf16), narrow(f32)],
        compiler_params=pltpu.CompilerParams(dimension_semantics=("arbitrary",),
                                             vmem_limit_bytes=VMEM_LIMIT),
        name="in_proj",
    )(x2, g, w, cos_t, sin_t)


def _retention_kernel(rq_ref, rk_ref, rv_ref, rg_ref, decay_ref, zeta_ref, xi_ref, cd_ref,
                      out_ref, state_ref):
    c = rq_ref.shape[0]

    @pl.when(pl.program_id(1) == 0)
    def _():
        state_ref[...] = jnp.zeros_like(state_ref)

    lane = lax.broadcasted_iota(jnp.int32, (c, LANES), 1)
    even = lane < HEAD_DIM
    r_i = lax.broadcasted_iota(jnp.int32, (LANES, LANES), 0)
    c_i = lax.broadcasted_iota(jnp.int32, (LANES, LANES), 1)
    same_head = (r_i < HEAD_DIM) == (c_i < HEAD_DIM)
    ones_bd = jnp.where(same_head, 1.0, 0.0).astype(bf16)

    for p in range(N_PAIRS):
        sl = slice(p * LANES, (p + 1) * LANES)
        q2 = rq_ref[:, sl]
        k2 = rk_ref[:, sl]
        v2 = rv_ref[:, sl]
        zero = jnp.zeros_like(q2)
        q_eo = jnp.concatenate([jnp.where(even, q2, zero), jnp.where(even, zero, q2)], axis=0)
        s_eo = _dot_nt(q_eo, k2) * decay_ref[2 * p:2 * p + 2].reshape(2 * c, c)
        pv = _dot(s_eo.astype(bf16), v2)
        inner = jnp.where(even, pv[0:c, :], pv[c:2 * c, :])
        r_prev = state_ref[p]
        cross = _dot(q2, r_prev.astype(bf16)) * xi_ref[p]
        o = inner + cross
        kz = (k2.astype(f32) * zeta_ref[p]).astype(bf16)
        s_new = jnp.where(same_head, _dot_tn(kz, v2), 0.0)
        state_ref[p] = cd_ref[p] * r_prev + s_new
        sq = o * o
        hi = sq.astype(bf16)
        lo = (sq - hi.astype(f32)).astype(bf16)
        sums = _dot(jnp.concatenate([hi, lo], axis=0), ones_bd)
        ms = (sums[0:c, :] + sums[c:2 * c, :]) * (1.0 / HEAD_DIM)
        o = o * lax.rsqrt(ms + EPS)
        out_ref[:, sl] = (o * _silu(rg_ref[:, sl])).astype(out_ref.dtype)


def _retention_tables(c):
    h = jnp.arange(RET_HEADS, dtype=f32)
    log_g = jnp.log(1.0 - 2.0 ** (-5.0 - h))
    pos = jnp.arange(c, dtype=f32)
    diff = pos[:, None] - pos[None, :]
    decay = jnp.where(diff >= 0, jnp.exp(log_g[:, None, None] * jnp.maximum(diff, 0.0)), 0.0)
    zeta = jnp.exp(log_g[:, None] * (c - 1.0 - pos)[None, :])
    xi = jnp.exp(log_g[:, None] * (pos + 1.0)[None, :])
    cd = jnp.exp(log_g * c)

    def pair_lanes(a):
        a = a.reshape(N_PAIRS, PAIR, c)
        return jnp.repeat(jnp.transpose(a, (0, 2, 1)), HEAD_DIM, axis=2)

    cd2 = jnp.repeat(cd.reshape(N_PAIRS, 1, PAIR), HEAD_DIM, axis=2)
    return decay, pair_lanes(zeta), pair_lanes(xi), cd2


def _retention(rq, rk, rv, rg, batch, seq):
    t = rq.shape[0]
    c = min(RET_CHUNK, seq)
    n = seq // c
    decay, zeta, xi, cd = _retention_tables(c)
    row = lambda b, i: (b * n + i, 0)
    fix3 = lambda b, i: (0, 0, 0)
    return pl.pallas_call(
        _retention_kernel,
        grid=(batch, n),
        in_specs=[pl.BlockSpec((c, GROUP), row)] * 4 + [
            pl.BlockSpec((RET_HEADS, c, c), fix3), pl.BlockSpec((N_PAIRS, c, LANES), fix3),
            pl.BlockSpec((N_PAIRS, c, LANES), fix3), pl.BlockSpec((N_PAIRS, 1, LANES), fix3)],
        out_specs=pl.BlockSpec((c, GROUP), row),
        out_shape=jax.ShapeDtypeStruct((t, GROUP), bf16),
        scratch_shapes=[pltpu.VMEM((N_PAIRS, LANES, LANES), f32)],
        compiler_params=pltpu.CompilerParams(dimension_semantics=("arbitrary", "arbitrary"),
                                             vmem_limit_bytes=VMEM_LIMIT),
        name="retention",
    )(rq, rk, rv, rg, decay, zeta, xi, cd)


def _key_to_float(key):
    bits = jnp.where(key >= 0, key, key ^ jnp.int32(0x7FFFFFFF))
    return lax.bitcast_convert_type(bits, f32)


def _dsa_kernel(k_sel, aq_ref, iq_ref, iw_ref, kk_ref, vv_ref, ik_ref, out_ref,
                lhs_a, lhs_i, xs, xb, bias, prod, p_ref, m_ref, alpha_ref, acc_ref, vaug, earlier):
    tq = aq_ref.shape[0]
    n_kc, kc, _ = xs.shape
    j = pl.program_id(1)
    n_chunks = ((j + 1) * tq + kc - 1) // kc
    kf = float(k_sel)

    @pl.when((pl.program_id(0) == 0) & (j == 0))
    def _():
        r_i = lax.broadcasted_iota(jnp.int32, (kc, kc), 0)
        c_i = lax.broadcasted_iota(jnp.int32, (kc, kc), 1)
        earlier[...] = jnp.where(c_i < r_i, 1.0, 0.0).astype(bf16)

    @pl.when(j == 0)
    def _():
        row = lax.broadcasted_iota(jnp.int32, (LANES, kc), 0)
        for c in range(n_kc):
            vt = vv_ref[c * kc:(c + 1) * kc, :].astype(f32).T
            vaug[c] = jnp.where(row < HEAD_DIM, vt, 1.0)[0:HEAD_DIM + SUM_ROWS, :].astype(bf16)

    lane = lax.broadcasted_iota(jnp.int32, (tq, LANES), 1)
    even = lane < HEAD_DIM
    for p in range(N_PAIRS):
        sl = slice(p * LANES, (p + 1) * LANES)
        a2 = aq_ref[:, sl]
        i2 = iq_ref[:, sl]
        zero = jnp.zeros_like(a2)
        lhs_a[(2 * p) * tq:(2 * p + 1) * tq, :] = jnp.where(even, a2, zero)
        lhs_a[(2 * p + 1) * tq:(2 * p + 2) * tq, :] = jnp.where(even, zero, a2)
        lhs_i[(2 * p) * tq:(2 * p + 1) * tq, :] = jnp.where(even, i2, zero)
        lhs_i[(2 * p + 1) * tq:(2 * p + 2) * tq, :] = jnp.where(even, zero, i2)
    w_heads = iw_ref[...].T[0:IDX_HEADS, :]

    key_row = lax.broadcasted_iota(jnp.int32, (kc, tq), 0)
    q_pos = j * tq + lax.broadcasted_iota(jnp.int32, (kc, tq), 1)

    def for_key_chunks(key_ref, lhs_ref, work):
        def one_chunk(c, _):
            start = pl.multiple_of(c * kc, kc)
            prod[...] = _dot_nt(key_ref[pl.ds(start, kc), :], lhs_ref[...])
            work(c, prod)
            return 0

        lax.fori_loop(0, n_chunks, one_chunk, 0)

    def score_chunk(c, prod):
        isc = jnp.zeros((kc, tq), f32)
        for h in range(IDX_HEADS):
            isc = isc + w_heads[h:h + 1, :] * jnp.maximum(prod[:, h * tq:(h + 1) * tq], 0.0)
        isc = jnp.where(c * kc + key_row <= q_pos, isc, -jnp.inf)
        xs[c] = isc
        xb[c] = isc.astype(xb.dtype)

    for_key_chunks(ik_ref, lhs_i, score_chunk)

    def count_where(pred):
        def body(c, accs):
            x = xs[c]
            accs = list(accs)
            for r in range(kc // SUBLANES):
                hit = jnp.where(pred(x[r * SUBLANES:(r + 1) * SUBLANES, :]), 1.0, 0.0)
                accs[r % N_COUNT_ACC] = accs[r % N_COUNT_ACC] + hit
            return tuple(accs)
        zero = jnp.zeros((SUBLANES, tq), f32)
        accs = lax.fori_loop(0, n_chunks, body, (zero,) * N_COUNT_ACC)
        total = functools.reduce(lambda a, b: a + b, accs)
        return jnp.broadcast_to(jnp.sum(total, axis=0, keepdims=True), (SUBLANES, tq))

    def count_ge(cand_f):
        return count_where(lambda x: x >= cand_f)

    def count_ge_packed(cand_b):
        one = jnp.ones((PACKED_ROWS, tq), xb.dtype)
        zero = jnp.zeros((PACKED_ROWS, tq), xb.dtype)

        def body(c, accs):
            x = xb[c]
            accs = list(accs)
            for r in range(kc // PACKED_ROWS):
                hit = jnp.where(x[r * PACKED_ROWS:(r + 1) * PACKED_ROWS, :] >= cand_b, one, zero)
                accs[r % N_COUNT_ACC] = accs[r % N_COUNT_ACC] + hit
            return tuple(accs)
        accs = lax.fori_loop(0, n_chunks, body, (zero,) * N_COUNT_ACC)
        total = functools.reduce(lambda a, b: a + b, [a.astype(f32) for a in accs])
        return jnp.broadcast_to(jnp.sum(total, axis=0, keepdims=True), (SUBLANES, tq))

    def key16_to_f32_bits(key16):
        b16 = jnp.where(key16 >= 0, key16, key16 ^ jnp.int32(0x7FFF)) & jnp.int32(0xFFFF)
        return lax.shift_left(b16, jnp.int32(16))

    def bit_step_packed(i, t16):
        cand = t16 + lax.shift_left(jnp.int32(1), jnp.int32(15) - i)
        cand_f = lax.bitcast_convert_type(key16_to_f32_bits(cand), f32)
        cand_b = jnp.broadcast_to(cand_f[0:1, :].astype(xb.dtype), (PACKED_ROWS, tq))
        return jnp.where(count_ge_packed(cand_b) >= kf, cand, t16)

    t16 = lax.fori_loop(0, 16, bit_step_packed, jnp.full((SUBLANES, tq), -2 ** 15, jnp.int32))

    c0_bits = key16_to_f32_bits(t16)
    c0_key = jnp.where(c0_bits >= 0, c0_bits, c0_bits ^ jnp.int32(0x7FFFFFFF))
    has_k = t16 > KEY16_NEG_INF

    def bit_step(i, carry):
        t_key, t_cnt = carry
        cand = t_key + lax.shift_left(jnp.int32(1), jnp.int32(16) - i)
        cnt = count_ge(_key_to_float(cand))
        ok = cnt >= kf
        return jnp.where(ok, cand, t_key), jnp.where(ok, cnt, t_cnt)

    t_key, t_cnt = lax.fori_loop(0, 17, bit_step, (c0_key - 2 ** 16, jnp.full((SUBLANES, tq), kf, f32)))
    t_key = jnp.where(has_k, t_key, INT_MIN)

    t_sel = _key_to_float(jnp.maximum(t_key, KEY_NEG_INF + 1))[0:1, :]

    def bias_chunk(c, _):
        bias[c] = jnp.where(xs[c] >= t_sel, 0.0, NEG_BIG).astype(bias.dtype)
        return 0

    lax.fori_loop(0, n_chunks, bias_chunk, 0)

    real = t_key > KEY_NEG_INF
    tied = (t_cnt > kf) & real

    @pl.when(jnp.max(jnp.where(tied, 1.0, 0.0)) > 0.0)
    def _():
        lo0 = _key_to_float(t_key)
        hi0 = _key_to_float(t_key + 1)
        n_hi0 = count_ge(hi0)
        n_eq = count_where(lambda x: x == lo0)
        unequal = tied & (n_eq != t_cnt - n_hi0)

        def refined(args):
            def refine(_, carry):
                lo, hi, n_hi = carry
                mid = lo + (hi - lo) * 0.5
                cnt = count_ge(mid)
                ok = cnt >= kf
                return jnp.where(ok, mid, lo), jnp.where(ok, hi, mid), jnp.where(ok, n_hi, cnt)
            return lax.fori_loop(0, N_REFINE, refine, args)

        lo8, hi8, n_hi8 = lax.cond(jnp.max(jnp.where(unequal, 1.0, 0.0)) > 0.0, refined, lambda args: args,
                                   (lo0, hi0, n_hi0))
        need8 = kf - n_hi8
        t_sel8 = _key_to_float(jnp.maximum(t_key, KEY_NEG_INF + 1))
        slabs = (kc // SUBLANES, SUBLANES, tq)

        def tie_chunk(c, seen8):
            x = xs[c].reshape(slabs)
            above = jnp.where(x >= hi8, 1.0, 0.0)
            tie = jnp.where(x >= lo8, 1.0, 0.0) - above
            before = _dot(earlier[...], tie.reshape(kc, tq).astype(bf16)).reshape(slabs)
            sel = above + jnp.where(before + seen8 < need8, tie, 0.0)
            sel = jnp.where(real, sel, jnp.where(x >= t_sel8, 1.0, 0.0))
            bias[c] = ((sel - 1.0) * -NEG_BIG).reshape(kc, tq).astype(bias.dtype)
            total = jnp.sum(jnp.sum(tie, axis=0), axis=0, keepdims=True)
            return seen8 + jnp.broadcast_to(total, (SUBLANES, tq))

        lax.fori_loop(0, n_chunks, tie_chunk, jnp.zeros((SUBLANES, tq), f32))

    m_ref[...] = jnp.full(m_ref.shape, -jnp.inf, f32)
    acc_ref[...] = jnp.zeros(acc_ref.shape, f32)

    def attend_chunk(c, prod):
        b = bias[c]
        for h in range(DSA_HEADS):
            hs = slice(h * tq, (h + 1) * tq)
            s = prod[:, hs].astype(bf16) + b
            m_old = m_ref[:, hs]
            m_new = jnp.maximum(m_old, jnp.max(s, axis=0, keepdims=True).astype(f32))
            alpha_ref[:, hs] = jnp.exp2(m_old - m_new)
            p_ref[:, hs] = jnp.exp2(s - m_new.astype(bf16))
            m_ref[:, hs] = m_new
        acc_ref[...] = alpha_ref[...] * acc_ref[...] + _dot(vaug[c], p_ref[...])

    for_key_chunks(kk_ref, lhs_a, attend_chunk)

    for p in range(N_PAIRS):
        halves = []
        for e in range(PAIR):
            hs = slice((PAIR * p + e) * tq, (PAIR * p + e + 1) * tq)
            halves.append(acc_ref[0:HEAD_DIM, hs] * (1.0 / acc_ref[HEAD_DIM:HEAD_DIM + 1, hs]))
        out_ref[:, p * LANES:(p + 1) * LANES] = jnp.concatenate(halves, axis=0).T.astype(out_ref.dtype)


def _dsa(aq, iq, iw, kk, vv, ik, batch, seq):
    t = aq.shape[0]
    tq = min(TQ, seq)
    kc = min(KC, seq)
    nq = seq // tq
    k_sel = min(TOPK_MAX, seq // 4)
    heads = DSA_HEADS
    qrow = lambda b, j: (b * nq + j, 0)
    brow = lambda b, j: (b, 0)
    return pl.pallas_call(
        functools.partial(_dsa_kernel, k_sel),
        grid=(batch, nq),
        in_specs=[pl.BlockSpec((tq, GROUP), qrow), pl.BlockSpec((tq, GROUP), qrow),
                  pl.BlockSpec((tq, LANES), qrow),
                  pl.BlockSpec((seq, LANES), brow), pl.BlockSpec((seq, LANES), brow),
                  pl.BlockSpec((seq, LANES), brow)],
        out_specs=pl.BlockSpec((tq, GROUP), qrow),
        out_shape=jax.ShapeDtypeStruct((t, GROUP), bf16),
        scratch_shapes=[
            pltpu.VMEM((heads * tq, LANES), bf16),
            pltpu.VMEM((heads * tq, LANES), bf16),
            pltpu.VMEM((seq // kc, kc, tq), f32),
            pltpu.VMEM((seq // kc, kc, tq), bf16),
            pltpu.VMEM((seq // kc, kc, tq), bf16),
            pltpu.VMEM((kc, heads * tq), f32),
            pltpu.VMEM((kc, heads * tq), bf16),
            pltpu.VMEM((1, heads * tq), f32),
            pltpu.VMEM((1, heads * tq), f32),
            pltpu.VMEM((HEAD_DIM + SUM_ROWS, heads * tq), f32),
            pltpu.VMEM((seq // kc, HEAD_DIM + SUM_ROWS, kc), bf16),
            pltpu.VMEM((kc, kc), bf16),
        ],
        compiler_params=pltpu.CompilerParams(dimension_semantics=("arbitrary", "arbitrary"),
                                             vmem_limit_bytes=VMEM_LIMIT),
        name="dsa",
    )(aq, iq, iw, kk, vv, ik)


def _outproj_kernel(ret_ref, att_ref, x_ref, w_ref, gpost_ref, gpre_ref, h_ref, hn_ref):
    mixed = _dot(ret_ref[...], w_ref[0:GROUP, :]) + _dot(att_ref[...], w_ref[GROUP:2 * GROUP, :])
    h = x_ref[...] + _rms(mixed, gpost_ref[...])
    h_ref[...] = h
    hn_ref[...] = _rms(h, gpre_ref[...]).astype(hn_ref.dtype)


def _outproj(ret, att, x2, w, g_post, g_pre, seq):
    t, d = x2.shape
    tm = min(TM_PROJ, seq)
    row = lambda i: (i, 0)
    fixed = lambda i: (0, 0)
    return pl.pallas_call(
        _outproj_kernel,
        grid=(t // tm,),
        in_specs=[pl.BlockSpec((tm, GROUP), row), pl.BlockSpec((tm, GROUP), row),
                  pl.BlockSpec((tm, d), row), pl.BlockSpec((2 * GROUP, d), fixed),
                  pl.BlockSpec((1, d), fixed), pl.BlockSpec((1, d), fixed)],
        out_specs=[pl.BlockSpec((tm, d), row), pl.BlockSpec((tm, d), row)],
        out_shape=[jax.ShapeDtypeStruct((t, d), f32), jax.ShapeDtypeStruct((t, d), bf16)],
        compiler_params=pltpu.CompilerParams(dimension_semantics=("arbitrary",),
                                             vmem_limit_bytes=VMEM_LIMIT),
        name="out_proj",
    )(ret, att, x2, w, g_post, g_pre)


def _ffn_kernel(hn_ref, h_ref, wg_ref, wu_ref, cwg_ref, cwu_ref, cbg_ref, cbu_ref, wd_ref, gpost_ref,
                out_ref, acc_ref, carry_ref):
    i = pl.program_id(1)
    f = pl.program_id(2)
    tm = hn_ref.shape[0]

    @pl.when(f == 0)
    def _():
        acc_ref[...] = jnp.zeros_like(acc_ref)

    @pl.when(i == 0)
    def _():
        carry_ref[f] = jnp.zeros(carry_ref.shape[1:], f32)

    hn = hn_ref[...]

    row = lax.broadcasted_iota(jnp.int32, (tm, wg_ref.shape[1]), 0)

    def conv(up, w_ref, b_ref, tail):
        m1 = jnp.where(row == 0, tail[7:8, :], pltpu.roll(up, 1, 0))
        m2 = jnp.where(row == 0, tail[6:7, :], jnp.where(row == 1, tail[7:8, :], pltpu.roll(up, 2, 0)))
        return b_ref[...] + m2 * w_ref[0:1, :] + m1 * w_ref[1:2, :] + up * w_ref[2:3, :]

    up_g = _dot(hn, wg_ref[...])
    up_u = _dot(hn, wu_ref[...])
    g = conv(up_g, cwg_ref, cbg_ref, carry_ref[f, 0])
    u = conv(up_u, cwu_ref, cbu_ref, carry_ref[f, 1])
    carry_ref[f, 0] = up_g[tm - 8:tm, :]
    carry_ref[f, 1] = up_u[tm - 8:tm, :]
    acc_ref[...] += _dot((_silu(g) * u).astype(bf16), wd_ref[...])

    @pl.when(f == pl.num_programs(2) - 1)
    def _():
        out_ref[...] = h_ref[...] + _rms(acc_ref[...], gpost_ref[...])


def _ffn(hn, h, w_up, conv_w, conv_b, w_down, g_post, batch, seq):
    t, d = h.shape
    tm = min(TM_FFN, seq)
    n_seq = seq // tm
    n_f = D_FF // TF
    resident = pl.Buffered(1) if n_f == 1 else None
    row = lambda b, i, f: (b * n_seq + i, 0)
    return pl.pallas_call(
        _ffn_kernel,
        grid=(batch, n_seq, n_f),
        in_specs=[pl.BlockSpec((tm, d), row), pl.BlockSpec((tm, d), row),
                  pl.BlockSpec((d, TF), lambda b, i, f: (0, f), pipeline_mode=resident),
                  pl.BlockSpec((d, TF), lambda b, i, f: (0, n_f + f), pipeline_mode=resident),
                  pl.BlockSpec((CONV_WIDTH, TF), lambda b, i, f: (0, f)),
                  pl.BlockSpec((CONV_WIDTH, TF), lambda b, i, f: (0, n_f + f)),
                  pl.BlockSpec((1, TF), lambda b, i, f: (0, f)),
                  pl.BlockSpec((1, TF), lambda b, i, f: (0, n_f + f)),
                  pl.BlockSpec((TF, d), lambda b, i, f: (f, 0), pipeline_mode=resident),
                  pl.BlockSpec((1, d), lambda b, i, f: (0, 0))],
        out_specs=pl.BlockSpec((tm, d), row),
        out_shape=jax.ShapeDtypeStruct((t, d), f32),
        scratch_shapes=[pltpu.VMEM((tm, d), f32), pltpu.VMEM((n_f, 2, SUBLANES, TF), f32)],
        compiler_params=pltpu.CompilerParams(dimension_semantics=("arbitrary", "arbitrary", "arbitrary"),
                                             vmem_limit_bytes=VMEM_LIMIT),
        name="ffn",
    )(hn, h, w_up, w_up, conv_w, conv_w, conv_b, conv_b, w_down, g_post)


def _relayout_w_in(w):
    d = w.shape[0]
    w = w.astype(bf16)
    sizes = (GROUP, GROUP, GROUP, GROUP, GROUP, HEAD_DIM, HEAD_DIM, IDX_HEADS * IDX_DIM, IDX_DIM, IDX_HEADS)
    parts, off = [], 0
    for n in sizes:
        parts.append(w[:, off:off + n])
        off += n
    rq, rk, rv, rg, aq, ak, av, iq, ik, iw = parts
    pad = jnp.zeros((d, LANES - IDX_HEADS), w.dtype)
    return jnp.concatenate([rq, rk, aq, iq, rv, rg, ak, ak, av, av, ik, ik, iw, pad], axis=1)


def _rope_tables(seq):
    half = HEAD_DIM // 2
    inv = ROPE_THETA ** (-jnp.arange(0, HEAD_DIM, 2, dtype=f32) / HEAD_DIM)
    ang = jnp.arange(seq, dtype=f32)[:, None] * inv[None, :]
    cos, sin = jnp.cos(ang), jnp.sin(ang)
    cos_t = jnp.tile(cos, (1, LANES // half))
    sin_t = jnp.tile(jnp.concatenate([-sin, sin], axis=1), (1, PAIR))
    return cos_t, sin_t


def kernel(x, mix_norm_pre, mix_norm_post, w_in, w_out, ffn_norm_pre, ffn_norm_post,
           w_up, conv_w, conv_b, w_down):
    batch, seq, d = x.shape
    depth = w_in.shape[0]
    cos_t, sin_t = _rope_tables(seq)
    h = x.reshape(batch * seq, d)
    for l in range(depth):
        rq, rk, aq, iq, rv, rg, kk, vv, ik, iw = _inproj(
            h, mix_norm_pre[l][None, :], _relayout_w_in(w_in[l]), cos_t, sin_t, seq)
        ret = _retention(rq, rk, rv, rg, batch, seq)
        att = _dsa(aq, iq, iw, kk, vv, ik, batch, seq)
        h, hn = _outproj(ret, att, h, w_out[l].astype(bf16), mix_norm_post[l][None, :],
                         ffn_norm_pre[l][None, :], seq)
        h = _ffn(hn, h, w_up[l].astype(bf16), conv_w[l], conv_b[l][None, :], w_down[l].astype(bf16),
                 ffn_norm_post[l][None, :], batch, seq)
    return h.reshape(batch, seq, d)
```

```python
import functools

import jax
import jax.numpy as jnp
from jax import lax
from jax.experimental import pallas as pl
from jax.experimental.pallas import tpu as pltpu

HEAD_DIM = 64
RET_HEADS = 8
DSA_HEADS = 8
IDX_HEADS = 8
IDX_DIM = 64
TOPK_MAX = 256
D_FF = 2816
CONV_WIDTH = 3
ROPE_THETA = 10000.0
EPS = 1e-6

LANES = 128
SUBLANES = 8
PAIR = LANES // HEAD_DIM
N_PAIRS = RET_HEADS // PAIR
GROUP = RET_HEADS * HEAD_DIM

_OFF_RQ, _OFF_RK, _OFF_AQ, _OFF_IQ, _OFF_RV, _OFF_RG = (i * GROUP for i in range(6))
_OFF_KK = 6 * GROUP
_OFF_VV = _OFF_KK + LANES
_OFF_IK = _OFF_VV + LANES
_OFF_IW = _OFF_IK + LANES
_W_TOTAL = _OFF_IW + LANES

TM_PROJ = 512
RET_CHUNK = 256
TQ = 256
KC = 512
TM_FFN = 512
VMEM_LIMIT = 48 * 1024 * 1024

NEG_BIG = -1e30
INT_MIN = -2 ** 31
KEY_NEG_INF = -2139095041
KEY16_NEG_INF = -32641
PACKED_ROWS = 16
N_COUNT_ACC = 4
N_REFINE = 12
SUM_ROWS = 16
LOG2E = 1.4426950408889634

f32 = jnp.float32
bf16 = jnp.bfloat16


def _dot(a, b):
    return jnp.dot(a, b, preferred_element_type=f32)


def _dot_nt(a, b):
    return lax.dot_general(a, b, (((1,), (1,)), ((), ())), preferred_element_type=f32)


def _dot_tn(a, b):
    return lax.dot_general(a, b, (((0,), (0,)), ((), ())), preferred_element_type=f32)


def _rms(x, g):
    return x * lax.rsqrt(jnp.mean(x * x, axis=-1, keepdims=True) + EPS) * g


def _silu(x):
    return x * (1.0 / (1.0 + jnp.exp(-x)))


def _inproj_kernel(x_ref, g_ref, w_ref, ch_ref, sh_ref,
                   rq_ref, rk_ref, aq_ref, iq_ref, rv_ref, rg_ref, kk_ref, vv_ref, ik_ref, iw_ref):
    tm = x_ref.shape[0]
    hn = _rms(x_ref[...], g_ref[...]).astype(bf16)
    cos = ch_ref[...]
    sin = sh_ref[...]
    lane = lax.broadcasted_iota(jnp.int32, (tm, LANES), 1)
    first_half = (lane % HEAD_DIM) < (HEAD_DIM // 2)

    def proj(off, width):
        return _dot(hn, w_ref[:, off:off + width])

    def rope(p):
        swapped = jnp.where(first_half, pltpu.roll(p, LANES - HEAD_DIM // 2, 1),
                            pltpu.roll(p, HEAD_DIM // 2, 1))
        return p * cos + swapped * sin

    def roped_group(off, out_ref, scale):
        p = proj(off, GROUP)
        for b in range(GROUP // LANES):
            r = rope(p[:, b * LANES:(b + 1) * LANES])
            if scale != 1.0:
                r = r * scale
            out_ref[:, b * LANES:(b + 1) * LANES] = r.astype(out_ref.dtype)

    roped_group(_OFF_RQ, rq_ref, 1.0)
    roped_group(_OFF_RK, rk_ref, HEAD_DIM ** -0.5)
    roped_group(_OFF_AQ, aq_ref, HEAD_DIM ** -0.5 * LOG2E)
    roped_group(_OFF_IQ, iq_ref, 1.0)
    rv_ref[...] = proj(_OFF_RV, GROUP).astype(bf16)
    rg_ref[...] = proj(_OFF_RG, GROUP)
    small = proj(_OFF_KK, 4 * LANES)
    kk_ref[...] = rope(small[:, 0:LANES]).astype(bf16)
    vv_ref[...] = small[:, LANES:2 * LANES].astype(bf16)
    ik_ref[...] = rope(small[:, 2 * LANES:3 * LANES]).astype(bf16)
    iw_ref[...] = small[:, 3 * LANES:4 * LANES] * (IDX_HEADS ** -0.5 * IDX_DIM ** -0.5)


def _inproj(x2, g, w, cos_t, sin_t, seq):
    t, d = x2.shape
    tm = min(TM_PROJ, seq)
    n_seq = seq // tm
    row = lambda i: (i, 0)
    fixed = lambda i: (0, 0)
    pos = lambda i: (i % n_seq, 0)
    wide = lambda dt: jax.ShapeDtypeStruct((t, GROUP), dt)
    narrow = lambda dt: jax.ShapeDtypeStruct((t, LANES), dt)
    return pl.pallas_call(
        _inproj_kernel,
        grid=(t // tm,),
        in_specs=[pl.BlockSpec((tm, d), row), pl.BlockSpec((1, d), fixed),
                  pl.BlockSpec((d, _W_TOTAL), fixed),
                  pl.BlockSpec((tm, LANES), pos), pl.BlockSpec((tm, LANES), pos)],
        out_specs=[pl.BlockSpec((tm, GROUP), row)] * 6 + [pl.BlockSpec((tm, LANES), row)] * 4,
        out_shape=[wide(bf16), wide(bf16), wide(bf16), wide(bf16), wide(bf16), wide(f32),
                   narrow(bf16), narrow(bf16), narrow(bf16), narrow(f32)],
        compiler_params=pltpu.CompilerParams(dimension_semantics=("arbitrary",),
                                             vmem_limit_bytes=VMEM_LIMIT),
        name="in_proj",
    )(x2, g, w, cos_t, sin_t)


def _retention_kernel(rq_ref, rk_ref, rv_ref, rg_ref, decay_ref, zeta_ref, xi_ref, cd_ref,
                      out_ref, state_ref):
    c = rq_ref.shape[0]

    @pl.when(pl.program_id(1) == 0)
    def _():
        state_ref[...] = jnp.zeros_like(state_ref)

    lane = lax.broadcasted_iota(jnp.int32, (c, LANES), 1)
    even = lane < HEAD_DIM
    r_i = lax.broadcasted_iota(jnp.int32, (LANES, LANES), 0)
    c_i = lax.broadcasted_iota(jnp.int32, (LANES, LANES), 1)
    same_head = (r_i < HEAD_DIM) == (c_i < HEAD_DIM)
    ones_bd = jnp.where(same_head, 1.0, 0.0).astype(bf16)

    for p in range(N_PAIRS):
        sl = slice(p * LANES, (p + 1) * LANES)
        q2 = rq_ref[:, sl]
        k2 = rk_ref[:, sl]
        v2 = rv_ref[:, sl]
        zero = jnp.zeros_like(q2)
        q_eo = jnp.concatenate([jnp.where(even, q2, zero), jnp.where(even, zero, q2)], axis=0)
        s_eo = _dot_nt(q_eo, k2) * decay_ref[2 * p:2 * p + 2].reshape(2 * c, c)
        pv = _dot(s_eo.astype(bf16), v2)
        inner = jnp.where(even, pv[0:c, :], pv[c:2 * c, :])
        r_prev = state_ref[p]
        cross = _dot(q2, r_prev.astype(bf16)) * xi_ref[p]
        o = inner + cross
        kz = (k2.astype(f32) * zeta_ref[p]).astype(bf16)
        s_new = jnp.where(same_head, _dot_tn(kz, v2), 0.0)
        state_ref[p] = cd_ref[p] * r_prev + s_new
        sq = o * o
        hi = sq.astype(bf16)
        lo = (sq - hi.astype(f32)).astype(bf16)
        sums = _dot(jnp.concatenate([hi, lo], axis=0), ones_bd)
        ms = (sums[0:c, :] + sums[c:2 * c, :]) * (1.0 / HEAD_DIM)
        o = o * lax.rsqrt(ms + EPS)
        out_ref[:, sl] = (o * _silu(rg_ref[:, sl])).astype(out_ref.dtype)


def _retention_tables(c):
    h = jnp.arange(RET_HEADS, dtype=f32)
    log_g = jnp.log(1.0 - 2.0 ** (-5.0 - h))
    pos = jnp.arange(c, dtype=f32)
    diff = pos[:, None] - pos[None, :]
    decay = jnp.where(diff >= 0, jnp.exp(log_g[:, None, None] * jnp.maximum(diff, 0.0)), 0.0)
    zeta = jnp.exp(log_g[:, None] * (c - 1.0 - pos)[None, :])
    xi = jnp.exp(log_g[:, None] * (pos + 1.0)[None, :])
    cd = jnp.exp(log_g * c)

    def pair_lanes(a):
        a = a.reshape(N_PAIRS, PAIR, c)
        return jnp.repeat(jnp.transpose(a, (0, 2, 1)), HEAD_DIM, axis=2)

    cd2 = jnp.repeat(cd.reshape(N_PAIRS, 1, PAIR), HEAD_DIM, axis=2)
    return decay, pair_lanes(zeta), pair_lanes(xi), cd2


def _retention(rq, rk, rv, rg, batch, seq):
    t = rq.shape[0]
    c = min(RET_CHUNK, seq)
    n = seq // c
    decay, zeta, xi, cd = _retention_tables(c)
    row = lambda b, i: (b * n + i, 0)
    fix3 = lambda b, i: (0, 0, 0)
    return pl.pallas_call(
        _retention_kernel,
        grid=(batch, n),
        in_specs=[pl.BlockSpec((c, GROUP), row)] * 4 + [
            pl.BlockSpec((RET_HEADS, c, c), fix3), pl.BlockSpec((N_PAIRS, c, LANES), fix3),
            pl.BlockSpec((N_PAIRS, c, LANES), fix3), pl.BlockSpec((N_PAIRS, 1, LANES), fix3)],
        out_specs=pl.BlockSpec((c, GROUP), row),
        out_shape=jax.ShapeDtypeStruct((t, GROUP), bf16),
        scratch_shapes=[pltpu.VMEM((N_PAIRS, LANES, LANES), f32)],
        compiler_params=pltpu.CompilerParams(dimension_semantics=("arbitrary", "arbitrary"),
                                             vmem_limit_bytes=VMEM_LIMIT),
        name="retention",
    )(rq, rk, rv, rg, decay, zeta, xi, cd)


def _key_to_float(key):
    bits = jnp.where(key >= 0, key, key ^ jnp.int32(0x7FFFFFFF))
    return lax.bitcast_convert_type(bits, f32)


def _dsa_kernel(k_sel, aq_ref, iq_ref, iw_ref, kk_ref, vv_ref, ik_ref, out_ref,
                lhs_a, lhs_i, xs, xb, bias, prod, p_ref, m_ref, alpha_ref, acc_ref, vaug, earlier):
    tq = aq_ref.shape[0]
    n_kc, kc, _ = xs.shape
    j = pl.program_id(1)
    n_chunks = ((j + 1) * tq + kc - 1) // kc
    kf = float(k_sel)

    @pl.when((pl.program_id(0) == 0) & (j == 0))
    def _():
        r_i = lax.broadcasted_iota(jnp.int32, (kc, kc), 0)
        c_i = lax.broadcasted_iota(jnp.int32, (kc, kc), 1)
        earlier[...] = jnp.where(c_i < r_i, 1.0, 0.0).astype(bf16)

    @pl.when(j == 0)
    def _():
        row = lax.broadcasted_iota(jnp.int32, (LANES, kc), 0)
        for c in range(n_kc):
            vt = vv_ref[c * kc:(c + 1) * kc, :].astype(f32).T
            vaug[c] = jnp.where(row < HEAD_DIM, vt, 1.0)[0:HEAD_DIM + SUM_ROWS, :].astype(bf16)

    lane = lax.broadcasted_iota(jnp.int32, (tq, LANES), 1)
    even = lane < HEAD_DIM
    for p in range(N_PAIRS):
        sl = slice(p * LANES, (p + 1) * LANES)
        a2 = aq_ref[:, sl]
        i2 = iq_ref[:, sl]
        zero = jnp.zeros_like(a2)
        lhs_a[(2 * p) * tq:(2 * p + 1) * tq, :] = jnp.where(even, a2, zero)
        lhs_a[(2 * p + 1) * tq:(2 * p + 2) * tq, :] = jnp.where(even, zero, a2)
        lhs_i[(2 * p) * tq:(2 * p + 1) * tq, :] = jnp.where(even, i2, zero)
        lhs_i[(2 * p + 1) * tq:(2 * p + 2) * tq, :] = jnp.where(even, zero, i2)
    w_heads = iw_ref[...].T[0:IDX_HEADS, :]

    key_row = lax.broadcasted_iota(jnp.int32, (kc, tq), 0)
    q_pos = j * tq + lax.broadcasted_iota(jnp.int32, (kc, tq), 1)

    def for_key_chunks(key_ref, lhs_ref, work):
        def one_chunk(c, _):
            start = pl.multiple_of(c * kc, kc)
            prod[...] = _dot_nt(key_ref[pl.ds(start, kc), :], lhs_ref[...])
            work(c, prod)
            return 0

        lax.fori_loop(0, n_chunks, one_chunk, 0)

    def score_chunk(c, prod):
        isc = jnp.zeros((kc, tq), f32)
        for h in range(IDX_HEADS):
            isc = isc + w_heads[h:h + 1, :] * jnp.maximum(prod[:, h * tq:(h + 1) * tq], 0.0)
        isc = jnp.where(c * kc + key_row <= q_pos, isc, -jnp.inf)
        xs[c] = isc
        xb[c] = isc.astype(xb.dtype)

    for_key_chunks(ik_ref, lhs_i, score_chunk)

    def count_where(pred):
        def body(c, accs):
            x = xs[c]
            accs = list(accs)
            for r in range(kc // SUBLANES):
                hit = jnp.where(pred(x[r * SUBLANES:(r + 1) * SUBLANES, :]), 1.0, 0.0)
                accs[r % N_COUNT_ACC] = accs[r % N_COUNT_ACC] + hit
            return tuple(accs)
        zero = jnp.zeros((SUBLANES, tq), f32)
        accs = lax.fori_loop(0, n_chunks, body, (zero,) * N_COUNT_ACC)
        total = functools.reduce(lambda a, b: a + b, accs)
        return jnp.broadcast_to(jnp.sum(total, axis=0, keepdims=True), (SUBLANES, tq))

    def count_ge(cand_f):
        return count_where(lambda x: x >= cand_f)

    def count_ge_packed(cand_b):
        one = jnp.ones((PACKED_ROWS, tq), xb.dtype)
        zero = jnp.zeros((PACKED_ROWS, tq), xb.dtype)

        def body(c, accs):
            x = xb[c]
            accs = list(accs)
            for r in range(kc // PACKED_ROWS):
                hit = jnp.where(x[r * PACKED_ROWS:(r + 1) * PACKED_ROWS, :] >= cand_b, one, zero)
                accs[r % N_COUNT_ACC] = accs[r % N_COUNT_ACC] + hit
            return tuple(accs)
        accs = lax.fori_loop(0, n_chunks, body, (zero,) * N_COUNT_ACC)
        total = functools.reduce(lambda a, b: a + b, [a.astype(f32) for a in accs])
        return jnp.broadcast_to(jnp.sum(total, axis=0, keepdims=True), (SUBLANES, tq))

    def key16_to_f32_bits(key16):
        b16 = jnp.where(key16 >= 0, key16, key16 ^ jnp.int32(0x7FFF)) & jnp.int32(0xFFFF)
        return lax.shift_left(b16, jnp.int32(16))

    def bit_step_packed(i, t16):
        cand = t16 + lax.shift_left(jnp.int32(1), jnp.int32(15) - i)
        cand_f = lax.bitcast_convert_type(key16_to_f32_bits(cand), f32)
        cand_b = jnp.broadcast_to(cand_f[0:1, :].astype(xb.dtype), (PACKED_ROWS, tq))
        return jnp.where(count_ge_packed(cand_b) >= kf, cand, t16)

    t16 = lax.fori_loop(0, 16, bit_step_packed, jnp.full((SUBLANES, tq), -2 ** 15, jnp.int32))

    c0_bits = key16_to_f32_bits(t16)
    c0_key = jnp.where(c0_bits >= 0, c0_bits, c0_bits ^ jnp.int32(0x7FFFFFFF))
    has_k = t16 > KEY16_NEG_INF

    def bit_step(i, carry):
        t_key, t_cnt = carry
        cand = t_key + lax.shift_left(jnp.int32(1), jnp.int32(16) - i)
        cnt = count_ge(_key_to_float(cand))
        ok = cnt >= kf
        return jnp.where(ok, cand, t_key), jnp.where(ok, cnt, t_cnt)

    t_key, t_cnt = lax.fori_loop(0, 17, bit_step, (c0_key - 2 ** 16, jnp.full((SUBLANES, tq), kf, f32)))
    t_key = jnp.where(has_k, t_key, INT_MIN)

    t_sel = _key_to_float(jnp.maximum(t_key, KEY_NEG_INF + 1))[0:1, :]

    def bias_chunk(c, _):
        bias[c] = jnp.where(xs[c] >= t_sel, 0.0, NEG_BIG).astype(bias.dtype)
        return 0

    lax.fori_loop(0, n_chunks, bias_chunk, 0)

    real = t_key > KEY_NEG_INF
    tied = (t_cnt > kf) & real

    @pl.when(jnp.max(jnp.where(tied, 1.0, 0.0)) > 0.0)
    def _():
        lo0 = _key_to_float(t_key)
        hi0 = _key_to_float(t_key + 1)
        n_hi0 = count_ge(hi0)
        n_eq = count_where(lambda x: x == lo0)
        unequal = tied & (n_eq != t_cnt - n_hi0)

        def refined(args):
            def refine(_, carry):
                lo, hi, n_hi = carry
                mid = lo + (hi - lo) * 0.5
                cnt = count_ge(mid)
                ok = cnt >= kf
                return jnp.where(ok, mid, lo), jnp.where(ok, hi, mid), jnp.where(ok, n_hi, cnt)
            return lax.fori_loop(0, N_REFINE, refine, args)

        lo8, hi8, n_hi8 = lax.cond(jnp.max(jnp.where(unequal, 1.0, 0.0)) > 0.0, refined, lambda args: args,
                                   (lo0, hi0, n_hi0))
        need8 = kf - n_hi8
        t_sel8 = _key_to_float(jnp.maximum(t_key, KEY_NEG_INF + 1))
        slabs = (kc // SUBLANES, SUBLANES, tq)

        def tie_chunk(c, seen8):
            x = xs[c].reshape(slabs)
            above = jnp.where(x >= hi8, 1.0, 0.0)
            tie = jnp.where(x >= lo8, 1.0, 0.0) - above
            before = _dot(earlier[...], tie.reshape(kc, tq).astype(bf16)).reshape(slabs)
            sel = above + jnp.where(before + seen8 < need8, tie, 0.0)
            sel = jnp.where(real, sel, jnp.where(x >= t_sel8, 1.0, 0.0))
            bias[c] = ((sel - 1.0) * -NEG_BIG).reshape(kc, tq).astype(bias.dtype)
            total = jnp.sum(jnp.sum(tie, axis=0), axis=0, keepdims=True)
            return seen8 + jnp.broadcast_to(total, (SUBLANES, tq))

        lax.fori_loop(0, n_chunks, tie_chunk, jnp.zeros((SUBLANES, tq), f32))

    m_ref[...] = jnp.full(m_ref.shape, -jnp.inf, f32)
    acc_ref[...] = jnp.zeros(acc_ref.shape, f32)

    def attend_chunk(c, prod):
        b = bias[c]
        for h in range(DSA_HEADS):
            hs = slice(h * tq, (h + 1) * tq)
            s = prod[:, hs].astype(bf16) + b
            m_old = m_ref[:, hs]
            m_new = jnp.maximum(m_old, jnp.max(s, axis=0, keepdims=True).astype(f32))
            alpha_ref[:, hs] = jnp.exp2(m_old - m_new)
            p_ref[:, hs] = jnp.exp2(s - m_new.astype(bf16))
            m_ref[:, hs] = m_new
        acc_ref[...] = alpha_ref[...] * acc_ref[...] + _dot(vaug[c], p_ref[...])

    for_key_chunks(kk_ref, lhs_a, attend_chunk)

    for p in range(N_PAIRS):
        halves = []
        for e in range(PAIR):
            hs = slice((PAIR * p + e) * tq, (PAIR * p + e + 1) * tq)
            halves.append(acc_ref[0:HEAD_DIM, hs] * (1.0 / acc_ref[HEAD_DIM:HEAD_DIM + 1, hs]))
        out_ref[:, p * LANES:(p + 1) * LANES] = jnp.concatenate(halves, axis=0).T.astype(out_ref.dtype)


def _dsa(aq, iq, iw, kk, vv, ik, batch, seq):
    t = aq.shape[0]
    tq = min(TQ, seq)
    kc = min(KC, seq)
    nq = seq // tq
    k_sel = min(TOPK_MAX, seq // 4)
    heads = DSA_HEADS
    qrow = lambda b, j: (b * nq + j, 0)
    brow = lambda b, j: (b, 0)
    return pl.pallas_call(
        functools.partial(_dsa_kernel, k_sel),
        grid=(batch, nq),
        in_specs=[pl.BlockSpec((tq, GROUP), qrow), pl.BlockSpec((tq, GROUP), qrow),
                  pl.BlockSpec((tq, LANES), qrow),
                  pl.BlockSpec((seq, LANES), brow), pl.BlockSpec((seq, LANES), brow),
                  pl.BlockSpec((seq, LANES), brow)],
        out_specs=pl.BlockSpec((tq, GROUP), qrow),
        out_shape=jax.ShapeDtypeStruct((t, GROUP), bf16),
        scratch_shapes=[
            pltpu.VMEM((heads * tq, LANES), bf16),
            pltpu.VMEM((heads * tq, LANES), bf16),
            pltpu.VMEM((seq // kc, kc, tq), f32),
            pltpu.VMEM((seq // kc, kc, tq), bf16),
            pltpu.VMEM((seq // kc, kc, tq), bf16),
            pltpu.VMEM((kc, heads * tq), f32),
            pltpu.VMEM((kc, heads * tq), bf16),
            pltpu.VMEM((1, heads * tq), f32),
            pltpu.VMEM((1, heads * tq), f32),
            pltpu.VMEM((HEAD_DIM + SUM_ROWS, heads * tq), f32),
            pltpu.VMEM((seq // kc, HEAD_DIM + SUM_ROWS, kc), bf16),
            pltpu.VMEM((kc, kc), bf16),
        ],
        compiler_params=pltpu.CompilerParams(dimension_semantics=("arbitrary", "arbitrary"),
                                             vmem_limit_bytes=VMEM_LIMIT),
        name="dsa",
    )(aq, iq, iw, kk, vv, ik)


def _mix_ffn_kernel(ret_ref, att_ref, x_ref, wo_ref, gmix_ref, gpre_ref, wg_ref, wu_ref, cwg_ref, cwu_ref,
                    cbg_ref, cbu_ref, wd_ref, gpost_ref, out_ref, carry_ref):
    tm = x_ref.shape[0]

    @pl.when(pl.program_id(1) == 0)
    def _():
        carry_ref[...] = jnp.zeros_like(carry_ref)

    mixed = _dot(ret_ref[...], wo_ref[0:GROUP, :]) + _dot(att_ref[...], wo_ref[GROUP:2 * GROUP, :])
    h = x_ref[...] + _rms(mixed, gmix_ref[...])

    hn = _rms(h, gpre_ref[...]).astype(bf16)
    row = lax.broadcasted_iota(jnp.int32, (tm, wg_ref.shape[1]), 0)

    def conv(part, w_ref, b_ref, up):
        tail = carry_ref[part]
        carry_ref[part] = up[tm - SUBLANES:tm, :]
        m1 = jnp.where(row == 0, tail[7:8, :], pltpu.roll(up, 1, 0))
        m2 = jnp.where(row == 0, tail[6:7, :], jnp.where(row == 1, tail[7:8, :], pltpu.roll(up, 2, 0)))
        return b_ref[...] + m2 * w_ref[0:1, :] + m1 * w_ref[1:2, :] + up * w_ref[2:3, :]

    g = conv(0, cwg_ref, cbg_ref, _dot(hn, wg_ref[...]))
    u = conv(1, cwu_ref, cbu_ref, _dot(hn, wu_ref[...]))
    ffn = _dot((_silu(g) * u).astype(bf16), wd_ref[...])
    out_ref[...] = h + _rms(ffn, gpost_ref[...])


def _mix_ffn(ret, att, x2, w_out, g_mix_post, g_ffn_pre, w_up, conv_w, conv_b, w_down, g_ffn_post, batch, seq):
    t, d = x2.shape
    tm = min(TM_FFN, seq)
    n_seq = seq // tm
    once = pl.Buffered(1)
    row = lambda b, i: (b * n_seq + i, 0)
    fixed = lambda b, i: (0, 0)
    second = lambda b, i: (0, 1)
    return pl.pallas_call(
        _mix_ffn_kernel,
        grid=(batch, n_seq),
        in_specs=[pl.BlockSpec((tm, GROUP), row), pl.BlockSpec((tm, GROUP), row), pl.BlockSpec((tm, d), row),
                  pl.BlockSpec((2 * GROUP, d), fixed, pipeline_mode=once),
                  pl.BlockSpec((1, d), fixed), pl.BlockSpec((1, d), fixed),
                  pl.BlockSpec((d, D_FF), fixed, pipeline_mode=once),
                  pl.BlockSpec((d, D_FF), second, pipeline_mode=once),
                  pl.BlockSpec((CONV_WIDTH, D_FF), fixed), pl.BlockSpec((CONV_WIDTH, D_FF), second),
                  pl.BlockSpec((1, D_FF), fixed), pl.BlockSpec((1, D_FF), second),
                  pl.BlockSpec((D_FF, d), fixed, pipeline_mode=once),
                  pl.BlockSpec((1, d), fixed)],
        out_specs=pl.BlockSpec((tm, d), row),
        out_shape=jax.ShapeDtypeStruct((t, d), f32),
        scratch_shapes=[pltpu.VMEM((2, SUBLANES, D_FF), f32)],
        compiler_params=pltpu.CompilerParams(dimension_semantics=("arbitrary", "arbitrary"),
                                             vmem_limit_bytes=VMEM_LIMIT),
        name="mix_ffn",
    )(ret, att, x2, w_out, g_mix_post, g_ffn_pre, w_up, w_up, conv_w, conv_w, conv_b, conv_b, w_down, g_ffn_post)


def _relayout_w_in(w):
    d = w.shape[0]
    w = w.astype(bf16)
    sizes = (GROUP, GROUP, GROUP, GROUP, GROUP, HEAD_DIM, HEAD_DIM, IDX_HEADS * IDX_DIM, IDX_DIM, IDX_HEADS)
    parts, off = [], 0
    for n in sizes:
        parts.append(w[:, off:off + n])
        off += n
    rq, rk, rv, rg, aq, ak, av, iq, ik, iw = parts
    pad = jnp.zeros((d, LANES - IDX_HEADS), w.dtype)
    return jnp.concatenate([rq, rk, aq, iq, rv, rg, ak, ak, av, av, ik, ik, iw, pad], axis=1)


def _rope_tables(seq):
    half = HEAD_DIM // 2
    inv = ROPE_THETA ** (-jnp.arange(0, HEAD_DIM, 2, dtype=f32) / HEAD_DIM)
    ang = jnp.arange(seq, dtype=f32)[:, None] * inv[None, :]
    cos, sin = jnp.cos(ang), jnp.sin(ang)
    cos_t = jnp.tile(cos, (1, LANES // half))
    sin_t = jnp.tile(jnp.concatenate([-sin, sin], axis=1), (1, PAIR))
    return cos_t, sin_t


def kernel(x, mix_norm_pre, mix_norm_post, w_in, w_out, ffn_norm_pre, ffn_norm_post,
           w_up, conv_w, conv_b, w_down):
    batch, seq, d = x.shape
    depth = w_in.shape[0]
    cos_t, sin_t = _rope_tables(seq)
    h = x.reshape(batch * seq, d)
    for l in range(depth):
        rq, rk, aq, iq, rv, rg, kk, vv, ik, iw = _inproj(
            h, mix_norm_pre[l][None, :], _relayout_w_in(w_in[l]), cos_t, sin_t, seq)
        ret = _retention(rq, rk, rv, rg, batch, seq)
        att = _dsa(aq, iq, iw, kk, vv, ik, batch, seq)
        h = _mix_ffn(ret, att, h, w_out[l].astype(bf16), mix_norm_post[l][None, :], ffn_norm_pre[l][None, :],
                     w_up[l].astype(bf16), conv_w[l], conv_b[l][None, :], w_down[l].astype(bf16),
                     ffn_norm_post[l][None, :], batch, seq)
    return h.reshape(batch, seq, d)
```

```python
import functools

import jax
import jax.numpy as jnp
from jax import lax
from jax.experimental import pallas as pl
from jax.experimental.pallas import tpu as pltpu

HEAD_DIM = 64
RET_HEADS = 8
DSA_HEADS = 8
IDX_HEADS = 8
IDX_DIM = 64
TOPK_MAX = 256
D_FF = 2816
CONV_WIDTH = 3
ROPE_THETA = 10000.0
EPS = 1e-6

LANES = 128
SUBLANES = 8
PAIR = LANES // HEAD_DIM
N_PAIRS = RET_HEADS // PAIR
GROUP = RET_HEADS * HEAD_DIM

_OFF_RQ, _OFF_RK, _OFF_AQ, _OFF_IQ, _OFF_RV, _OFF_RG = (i * GROUP for i in range(6))
_OFF_KK = 6 * GROUP
_OFF_VV = _OFF_KK + LANES
_OFF_IK = _OFF_VV + LANES
_OFF_IW = _OFF_IK + LANES
_W_TOTAL = _OFF_IW + LANES

TM_PROJ = 512
RET_CHUNK = 256
TQ = 256
KC = 512
TM_FFN = 512
VMEM_LIMIT = 48 * 1024 * 1024

NEG_BIG = -1e30
INT_MIN = -2 ** 31
KEY_NEG_INF = -2139095041
KEY16_NEG_INF = -32641
PACKED_ROWS = 16
N_COUNT_ACC = 4
N_REFINE = 12
SUM_ROWS = 16
LOG2E = 1.4426950408889634

f32 = jnp.float32
bf16 = jnp.bfloat16


def _dot(a, b):
    return jnp.dot(a, b, preferred_element_type=f32)


def _dot_nt(a, b):
    return lax.dot_general(a, b, (((1,), (1,)), ((), ())), preferred_element_type=f32)


def _dot_tn(a, b):
    return lax.dot_general(a, b, (((0,), (0,)), ((), ())), preferred_element_type=f32)


def _rms(x, g):
    return x * lax.rsqrt(jnp.mean(x * x, axis=-1, keepdims=True) + EPS) * g


def _silu(x):
    return x * (1.0 / (1.0 + jnp.exp(-x)))


def _inproj_kernel(x_ref, g_ref, w_ref, ch_ref, sh_ref,
                   rq_ref, rk_ref, aq_ref, iq_ref, rv_ref, rg_ref, kk_ref, vv_ref, ik_ref, iw_ref):
    tm = x_ref.shape[0]
    hn = _rms(x_ref[...], g_ref[...]).astype(bf16)
    cos = ch_ref[...]
    sin = sh_ref[...]
    lane = lax.broadcasted_iota(jnp.int32, (tm, LANES), 1)
    first_half = (lane % HEAD_DIM) < (HEAD_DIM // 2)

    def proj(off, width):
        return _dot(hn, w_ref[:, off:off + width])

    def rope(p):
        swapped = jnp.where(first_half, pltpu.roll(p, LANES - HEAD_DIM // 2, 1),
                            pltpu.roll(p, HEAD_DIM // 2, 1))
        return p * cos + swapped * sin

    def roped_group(off, out_ref, scale):
        p = proj(off, GROUP)
        for b in range(GROUP // LANES):
            r = rope(p[:, b * LANES:(b + 1) * LANES])
            if scale != 1.0:
                r = r * scale
            out_ref[:, b * LANES:(b + 1) * LANES] = r.astype(out_ref.dtype)

    roped_group(_OFF_RQ, rq_ref, 1.0)
    roped_group(_OFF_RK, rk_ref, HEAD_DIM ** -0.5)
    roped_group(_OFF_AQ, aq_ref, HEAD_DIM ** -0.5 * LOG2E)
    roped_group(_OFF_IQ, iq_ref, 1.0)
    rv_ref[...] = proj(_OFF_RV, GROUP).astype(bf16)
    rg_ref[...] = proj(_OFF_RG, GROUP)
    small = proj(_OFF_KK, 4 * LANES)
    kk_ref[...] = rope(small[:, 0:LANES]).astype(bf16)
    vv_ref[...] = small[:, LANES:2 * LANES].astype(bf16)
    ik_ref[...] = rope(small[:, 2 * LANES:3 * LANES]).astype(bf16)
    iw_ref[...] = small[:, 3 * LANES:4 * LANES] * (IDX_HEADS ** -0.5 * IDX_DIM ** -0.5)


def _inproj(x2, g, w, cos_t, sin_t, seq):
    t, d = x2.shape
    tm = min(TM_PROJ, seq)
    n_seq = seq // tm
    row = lambda i: (i, 0)
    fixed = lambda i: (0, 0)
    pos = lambda i: (i % n_seq, 0)
    wide = lambda dt: jax.ShapeDtypeStruct((t, GROUP), dt)
    narrow = lambda dt: jax.ShapeDtypeStruct((t, LANES), dt)
    return pl.pallas_call(
        _inproj_kernel,
        grid=(t // tm,),
        in_specs=[pl.BlockSpec((tm, d), row), pl.BlockSpec((1, d), fixed),
                  pl.BlockSpec((d, _W_TOTAL), fixed),
                  pl.BlockSpec((tm, LANES), pos), pl.BlockSpec((tm, LANES), pos)],
        out_specs=[pl.BlockSpec((tm, GROUP), row)] * 6 + [pl.BlockSpec((tm, LANES), row)] * 4,
        out_shape=[wide(bf16), wide(bf16), wide(bf16), wide(bf16), wide(bf16), wide(f32),
                   narrow(bf16), narrow(bf16), narrow(bf16), narrow(f32)],
        compiler_params=pltpu.CompilerParams(dimension_semantics=("arbitrary",),
                                             vmem_limit_bytes=VMEM_LIMIT),
        name="in_proj",
    )(x2, g, w, cos_t, sin_t)


def _retention_kernel(rq_ref, rk_ref, rv_ref, rg_ref, decay_ref, zeta_ref, xi_ref, cd_ref,
                      out_ref, state_ref):
    c = rq_ref.shape[0]

    @pl.when(pl.program_id(1) == 0)
    def _():
        state_ref[...] = jnp.zeros_like(state_ref)

    lane = lax.broadcasted_iota(jnp.int32, (c, LANES), 1)
    even = lane < HEAD_DIM
    r_i = lax.broadcasted_iota(jnp.int32, (LANES, LANES), 0)
    c_i = lax.broadcasted_iota(jnp.int32, (LANES, LANES), 1)
    same_head = (r_i < HEAD_DIM) == (c_i < HEAD_DIM)
    ones_bd = jnp.where(same_head, 1.0, 0.0).astype(bf16)

    for p in range(N_PAIRS):
        sl = slice(p * LANES, (p + 1) * LANES)
        q2 = rq_ref[:, sl]
        k2 = rk_ref[:, sl]
        v2 = rv_ref[:, sl]
        zero = jnp.zeros_like(q2)
        q_eo = jnp.concatenate([jnp.where(even, q2, zero), jnp.where(even, zero, q2)], axis=0)
        s_eo = _dot_nt(q_eo, k2) * decay_ref[2 * p:2 * p + 2].reshape(2 * c, c)
        pv = _dot(s_eo.astype(bf16), v2)
        inner = jnp.where(even, pv[0:c, :], pv[c:2 * c, :])
        r_prev = state_ref[p]
        cross = _dot(q2, r_prev.astype(bf16)) * xi_ref[p]
        o = inner + cross
        kz = (k2.astype(f32) * zeta_ref[p]).astype(bf16)
        s_new = jnp.where(same_head, _dot_tn(kz, v2), 0.0)
        state_ref[p] = cd_ref[p] * r_prev + s_new
        sq = o * o
        hi = sq.astype(bf16)
        lo = (sq - hi.astype(f32)).astype(bf16)
        sums = _dot(jnp.concatenate([hi, lo], axis=0), ones_bd)
        ms = (sums[0:c, :] + sums[c:2 * c, :]) * (1.0 / HEAD_DIM)
        o = o * lax.rsqrt(ms + EPS)
        out_ref[:, sl] = (o * _silu(rg_ref[:, sl])).astype(out_ref.dtype)


def _retention_tables(c):
    h = jnp.arange(RET_HEADS, dtype=f32)
    log_g = jnp.log(1.0 - 2.0 ** (-5.0 - h))
    pos = jnp.arange(c, dtype=f32)
    diff = pos[:, None] - pos[None, :]
    decay = jnp.where(diff >= 0, jnp.exp(log_g[:, None, None] * jnp.maximum(diff, 0.0)), 0.0)
    zeta = jnp.exp(log_g[:, None] * (c - 1.0 - pos)[None, :])
    xi = jnp.exp(log_g[:, None] * (pos + 1.0)[None, :])
    cd = jnp.exp(log_g * c)

    def pair_lanes(a):
        a = a.reshape(N_PAIRS, PAIR, c)
        return jnp.repeat(jnp.transpose(a, (0, 2, 1)), HEAD_DIM, axis=2)

    cd2 = jnp.repeat(cd.reshape(N_PAIRS, 1, PAIR), HEAD_DIM, axis=2)
    return decay, pair_lanes(zeta), pair_lanes(xi), cd2


def _retention(rq, rk, rv, rg, batch, seq):
    t = rq.shape[0]
    c = min(RET_CHUNK, seq)
    n = seq // c
    decay, zeta, xi, cd = _retention_tables(c)
    row = lambda b, i: (b * n + i, 0)
    fix3 = lambda b, i: (0, 0, 0)
    return pl.pallas_call(
        _retention_kernel,
        grid=(batch, n),
        in_specs=[pl.BlockSpec((c, GROUP), row)] * 4 + [
            pl.BlockSpec((RET_HEADS, c, c), fix3), pl.BlockSpec((N_PAIRS, c, LANES), fix3),
            pl.BlockSpec((N_PAIRS, c, LANES), fix3), pl.BlockSpec((N_PAIRS, 1, LANES), fix3)],
        out_specs=pl.BlockSpec((c, GROUP), row),
        out_shape=jax.ShapeDtypeStruct((t, GROUP), bf16),
        scratch_shapes=[pltpu.VMEM((N_PAIRS, LANES, LANES), f32)],
        compiler_params=pltpu.CompilerParams(dimension_semantics=("arbitrary", "arbitrary"),
                                             vmem_limit_bytes=VMEM_LIMIT),
        name="retention",
    )(rq, rk, rv, rg, decay, zeta, xi, cd)


def _key_to_float(key):
    bits = jnp.where(key >= 0, key, key ^ jnp.int32(0x7FFFFFFF))
    return lax.bitcast_convert_type(bits, f32)


def _dsa_kernel(k_sel, aq_ref, iq_ref, iw_ref, kk_ref, vv_ref, ik_ref, out_ref,
                lhs_a, lhs_i, xs, xb, bias, prod, p_ref, m_ref, alpha_ref, acc_ref, vaug, earlier):
    tq = aq_ref.shape[0]
    n_kc, kc, _ = xs.shape
    j = pl.program_id(1)
    half = kc // 2
    n_keys = (j + 1) * tq
    n_full = n_keys // kc
    has_half = n_keys % kc != 0
    kf = float(k_sel)

    def over_chunks(body, init):
        carry = lax.fori_loop(0, n_full, lambda c, a: body(c, kc, a), init)
        return lax.cond(has_half, lambda a: body(n_full, half, a), lambda a: a, carry)

    @pl.when((pl.program_id(0) == 0) & (j == 0))
    def _():
        r_i = lax.broadcasted_iota(jnp.int32, (kc, kc), 0)
        c_i = lax.broadcasted_iota(jnp.int32, (kc, kc), 1)
        earlier[...] = jnp.where(c_i < r_i, 1.0, 0.0).astype(bf16)

    @pl.when(j == 0)
    def _():
        row = lax.broadcasted_iota(jnp.int32, (LANES, kc), 0)
        for c in range(n_kc):
            vt = vv_ref[c * kc:(c + 1) * kc, :].astype(f32).T
            vaug[c] = jnp.where(row < HEAD_DIM, vt, 1.0)[0:HEAD_DIM + SUM_ROWS, :].astype(bf16)

    lane = lax.broadcasted_iota(jnp.int32, (tq, LANES), 1)
    even = lane < HEAD_DIM
    for p in range(N_PAIRS):
        sl = slice(p * LANES, (p + 1) * LANES)
        a2 = aq_ref[:, sl]
        i2 = iq_ref[:, sl]
        zero = jnp.zeros_like(a2)
        lhs_a[(2 * p) * tq:(2 * p + 1) * tq, :] = jnp.where(even, a2, zero)
        lhs_a[(2 * p + 1) * tq:(2 * p + 2) * tq, :] = jnp.where(even, zero, a2)
        lhs_i[(2 * p) * tq:(2 * p + 1) * tq, :] = jnp.where(even, i2, zero)
        lhs_i[(2 * p + 1) * tq:(2 * p + 2) * tq, :] = jnp.where(even, zero, i2)
    w_heads = iw_ref[...].T[0:IDX_HEADS, :]

    def for_key_chunks(key_ref, lhs_ref, work):
        def one_chunk(c, rows, _):
            start = pl.multiple_of(c * kc, half)
            prod[0:rows, :] = _dot_nt(key_ref[pl.ds(start, rows), :], lhs_ref[...])
            work(c, rows)
            return 0

        over_chunks(one_chunk, 0)

    def score_chunk(c, rows):
        isc = jnp.zeros((rows, tq), f32)
        for h in range(IDX_HEADS):
            isc = isc + w_heads[h:h + 1, :] * jnp.maximum(prod[0:rows, h * tq:(h + 1) * tq], 0.0)
        key_pos = c * kc + lax.broadcasted_iota(jnp.int32, (rows, tq), 0)
        q_pos = n_keys - tq + lax.broadcasted_iota(jnp.int32, (rows, tq), 1)
        isc = jnp.where(key_pos <= q_pos, isc, -jnp.inf)
        xs[c, 0:rows, :] = isc
        xb[c, 0:rows, :] = isc.astype(xb.dtype)

    for_key_chunks(ik_ref, lhs_i, score_chunk)

    def count_where(pred):
        def body(c, rows, accs):
            x = xs[c, 0:rows, :]
            accs = list(accs)
            for r in range(rows // SUBLANES):
                hit = jnp.where(pred(x[r * SUBLANES:(r + 1) * SUBLANES, :]), 1.0, 0.0)
                accs[r % N_COUNT_ACC] = accs[r % N_COUNT_ACC] + hit
            return tuple(accs)
        zero = jnp.zeros((SUBLANES, tq), f32)
        accs = over_chunks(body, (zero,) * N_COUNT_ACC)
        total = functools.reduce(lambda a, b: a + b, accs)
        return jnp.broadcast_to(jnp.sum(total, axis=0, keepdims=True), (SUBLANES, tq))

    def count_ge(cand_f):
        return count_where(lambda x: x >= cand_f)

    def count_ge_packed(cand_b):
        one = jnp.ones((PACKED_ROWS, tq), xb.dtype)
        zero = jnp.zeros((PACKED_ROWS, tq), xb.dtype)

        def body(c, rows, accs):
            x = xb[c, 0:rows, :]
            accs = list(accs)
            for r in range(rows // PACKED_ROWS):
                hit = jnp.where(x[r * PACKED_ROWS:(r + 1) * PACKED_ROWS, :] >= cand_b, one, zero)
                accs[r % N_COUNT_ACC] = accs[r % N_COUNT_ACC] + hit
            return tuple(accs)
        accs = over_chunks(body, (zero,) * N_COUNT_ACC)
        total = functools.reduce(lambda a, b: a + b, [a.astype(f32) for a in accs])
        return jnp.broadcast_to(jnp.sum(total, axis=0, keepdims=True), (SUBLANES, tq))

    def key16_to_f32_bits(key16):
        b16 = jnp.where(key16 >= 0, key16, key16 ^ jnp.int32(0x7FFF)) & jnp.int32(0xFFFF)
        return lax.shift_left(b16, jnp.int32(16))

    def bit_step_packed(i, t16):
        cand = t16 + lax.shift_left(jnp.int32(1), jnp.int32(15) - i)
        cand_f = lax.bitcast_convert_type(key16_to_f32_bits(cand), f32)
        cand_b = jnp.broadcast_to(cand_f[0:1, :].astype(xb.dtype), (PACKED_ROWS, tq))
        return jnp.where(count_ge_packed(cand_b) >= kf, cand, t16)

    t16 = lax.fori_loop(0, 16, bit_step_packed, jnp.full((SUBLANES, tq), -2 ** 15, jnp.int32))

    c0_bits = key16_to_f32_bits(t16)
    c0_key = jnp.where(c0_bits >= 0, c0_bits, c0_bits ^ jnp.int32(0x7FFFFFFF))
    has_k = t16 > KEY16_NEG_INF

    def bit_step(i, carry):
        t_key, t_cnt = carry
        cand = t_key + lax.shift_left(jnp.int32(1), jnp.int32(16) - i)
        cnt = count_ge(_key_to_float(cand))
        ok = cnt >= kf
        return jnp.where(ok, cand, t_key), jnp.where(ok, cnt, t_cnt)

    t_key, t_cnt = lax.fori_loop(0, 17, bit_step, (c0_key - 2 ** 16, jnp.full((SUBLANES, tq), kf, f32)))
    t_key = jnp.where(has_k, t_key, INT_MIN)

    t_sel = _key_to_float(jnp.maximum(t_key, KEY_NEG_INF + 1))[0:1, :]

    def bias_chunk(c, rows, _):
        bias[c, 0:rows, :] = jnp.where(xs[c, 0:rows, :] >= t_sel, 0.0, NEG_BIG).astype(bias.dtype)
        return 0

    over_chunks(bias_chunk, 0)

    real = t_key > KEY_NEG_INF
    tied = (t_cnt > kf) & real

    @pl.when(jnp.max(jnp.where(tied, 1.0, 0.0)) > 0.0)
    def _():
        lo0 = _key_to_float(t_key)
        hi0 = _key_to_float(t_key + 1)
        n_hi0 = count_ge(hi0)
        n_eq = count_where(lambda x: x == lo0)
        unequal = tied & (n_eq != t_cnt - n_hi0)

        def refined(args):
            def refine(_, carry):
                lo, hi, n_hi = carry
                mid = lo + (hi - lo) * 0.5
                cnt = count_ge(mid)
                ok = cnt >= kf
                return jnp.where(ok, mid, lo), jnp.where(ok, hi, mid), jnp.where(ok, n_hi, cnt)
            return lax.fori_loop(0, N_REFINE, refine, args)

        lo8, hi8, n_hi8 = lax.cond(jnp.max(jnp.where(unequal, 1.0, 0.0)) > 0.0, refined, lambda args: args,
                                   (lo0, hi0, n_hi0))
        need8 = kf - n_hi8
        t_sel8 = _key_to_float(jnp.maximum(t_key, KEY_NEG_INF + 1))

        def tie_chunk(c, rows, seen8):
            slabs = (rows // SUBLANES, SUBLANES, tq)
            x = xs[c, 0:rows, :].reshape(slabs)
            above = jnp.where(x >= hi8, 1.0, 0.0)
            tie = jnp.where(x >= lo8, 1.0, 0.0) - above
            before = _dot(earlier[0:rows, 0:rows], tie.reshape(rows, tq).astype(bf16)).reshape(slabs)
            sel = above + jnp.where(before + seen8 < need8, tie, 0.0)
            sel = jnp.where(real, sel, jnp.where(x >= t_sel8, 1.0, 0.0))
            bias[c, 0:rows, :] = ((sel - 1.0) * -NEG_BIG).reshape(rows, tq).astype(bias.dtype)
            total = jnp.sum(jnp.sum(tie, axis=0), axis=0, keepdims=True)
            return seen8 + jnp.broadcast_to(total, (SUBLANES, tq))

        over_chunks(tie_chunk, jnp.zeros((SUBLANES, tq), f32))

    m_ref[...] = jnp.full(m_ref.shape, -jnp.inf, f32)
    acc_ref[...] = jnp.zeros(acc_ref.shape, f32)

    def attend_chunk(c, rows):
        b = bias[c, 0:rows, :]
        for h in range(DSA_HEADS):
            hs = slice(h * tq, (h + 1) * tq)
            s = prod[0:rows, hs].astype(bf16) + b
            m_old = m_ref[:, hs]
            m_new = jnp.maximum(m_old, jnp.max(s, axis=0, keepdims=True).astype(f32))
            alpha_ref[:, hs] = jnp.exp2(m_old - m_new)
            p_ref[0:rows, hs] = jnp.exp2(s - m_new.astype(bf16))
            m_ref[:, hs] = m_new
        acc_ref[...] = alpha_ref[...] * acc_ref[...] + _dot(vaug[c, :, 0:rows], p_ref[0:rows, :])

    for_key_chunks(kk_ref, lhs_a, attend_chunk)

    for p in range(N_PAIRS):
        halves = []
        for e in range(PAIR):
            hs = slice((PAIR * p + e) * tq, (PAIR * p + e + 1) * tq)
            halves.append(acc_ref[0:HEAD_DIM, hs] * (1.0 / acc_ref[HEAD_DIM:HEAD_DIM + 1, hs]))
        out_ref[:, p * LANES:(p + 1) * LANES] = jnp.concatenate(halves, axis=0).T.astype(out_ref.dtype)


def _dsa(aq, iq, iw, kk, vv, ik, batch, seq):
    t = aq.shape[0]
    tq = min(TQ, seq)
    kc = min(KC, seq)
    assert kc in (tq, 2 * tq), "a query tile must end on a whole or a half key chunk"
    nq = seq // tq
    k_sel = min(TOPK_MAX, seq // 4)
    heads = DSA_HEADS
    qrow = lambda b, j: (b * nq + j, 0)
    brow = lambda b, j: (b, 0)
    return pl.pallas_call(
        functools.partial(_dsa_kernel, k_sel),
        grid=(batch, nq),
        in_specs=[pl.BlockSpec((tq, GROUP), qrow), pl.BlockSpec((tq, GROUP), qrow),
                  pl.BlockSpec((tq, LANES), qrow),
                  pl.BlockSpec((seq, LANES), brow), pl.BlockSpec((seq, LANES), brow),
                  pl.BlockSpec((seq, LANES), brow)],
        out_specs=pl.BlockSpec((tq, GROUP), qrow),
        out_shape=jax.ShapeDtypeStruct((t, GROUP), bf16),
        scratch_shapes=[
            pltpu.VMEM((heads * tq, LANES), bf16),
            pltpu.VMEM((heads * tq, LANES), bf16),
            pltpu.VMEM((seq // kc, kc, tq), f32),
            pltpu.VMEM((seq // kc, kc, tq), bf16),
            pltpu.VMEM((seq // kc, kc, tq), bf16),
            pltpu.VMEM((kc, heads * tq), f32),
            pltpu.VMEM((kc, heads * tq), bf16),
            pltpu.VMEM((1, heads * tq), f32),
            pltpu.VMEM((1, heads * tq), f32),
            pltpu.VMEM((HEAD_DIM + SUM_ROWS, heads * tq), f32),
            pltpu.VMEM((seq // kc, HEAD_DIM + SUM_ROWS, kc), bf16),
            pltpu.VMEM((kc, kc), bf16),
        ],
        compiler_params=pltpu.CompilerParams(dimension_semantics=("arbitrary", "arbitrary"),
                                             vmem_limit_bytes=VMEM_LIMIT),
        name="dsa",
    )(aq, iq, iw, kk, vv, ik)


def _mix_ffn_kernel(ret_ref, att_ref, x_ref, wo_ref, gmix_ref, gpre_ref, wg_ref, wu_ref, cwg_ref, cwu_ref,
                    cbg_ref, cbu_ref, wd_ref, gpost_ref, out_ref, carry_ref):
    tm = x_ref.shape[0]

    @pl.when(pl.program_id(1) == 0)
    def _():
        carry_ref[...] = jnp.zeros_like(carry_ref)

    mixed = _dot(ret_ref[...], wo_ref[0:GROUP, :]) + _dot(att_ref[...], wo_ref[GROUP:2 * GROUP, :])
    h = x_ref[...] + _rms(mixed, gmix_ref[...])

    hn = _rms(h, gpre_ref[...]).astype(bf16)
    row = lax.broadcasted_iota(jnp.int32, (tm, wg_ref.shape[1]), 0)

    def conv(part, w_ref, b_ref, up):
        tail = carry_ref[part]
        carry_ref[part] = up[tm - SUBLANES:tm, :]
        m1 = jnp.where(row == 0, tail[7:8, :], pltpu.roll(up, 1, 0))
        m2 = jnp.where(row == 0, tail[6:7, :], jnp.where(row == 1, tail[7:8, :], pltpu.roll(up, 2, 0)))
        return b_ref[...] + m2 * w_ref[0:1, :] + m1 * w_ref[1:2, :] + up * w_ref[2:3, :]

    g = conv(0, cwg_ref, cbg_ref, _dot(hn, wg_ref[...]))
    u = conv(1, cwu_ref, cbu_ref, _dot(hn, wu_ref[...]))
    ffn = _dot((_silu(g) * u).astype(bf16), wd_ref[...])
    out_ref[...] = h + _rms(ffn, gpost_ref[...])


def _mix_ffn(ret, att, x2, w_out, g_mix_post, g_ffn_pre, w_up, conv_w, conv_b, w_down, g_ffn_post, batch, seq):
    t, d = x2.shape
    tm = min(TM_FFN, seq)
    n_seq = seq // tm
    once = pl.Buffered(1)
    row = lambda b, i: (b * n_seq + i, 0)
    fixed = lambda b, i: (0, 0)
    second = lambda b, i: (0, 1)
    return pl.pallas_call(
        _mix_ffn_kernel,
        grid=(batch, n_seq),
        in_specs=[pl.BlockSpec((tm, GROUP), row), pl.BlockSpec((tm, GROUP), row), pl.BlockSpec((tm, d), row),
                  pl.BlockSpec((2 * GROUP, d), fixed, pipeline_mode=once),
                  pl.BlockSpec((1, d), fixed), pl.BlockSpec((1, d), fixed),
                  pl.BlockSpec((d, D_FF), fixed, pipeline_mode=once),
                  pl.BlockSpec((d, D_FF), second, pipeline_mode=once),
                  pl.BlockSpec((CONV_WIDTH, D_FF), fixed), pl.BlockSpec((CONV_WIDTH, D_FF), second),
                  pl.BlockSpec((1, D_FF), fixed), pl.BlockSpec((1, D_FF), second),
                  pl.BlockSpec((D_FF, d), fixed, pipeline_mode=once),
                  pl.BlockSpec((1, d), fixed)],
        out_specs=pl.BlockSpec((tm, d), row),
        out_shape=jax.ShapeDtypeStruct((t, d), f32),
        scratch_shapes=[pltpu.VMEM((2, SUBLANES, D_FF), f32)],
        compiler_params=pltpu.CompilerParams(dimension_semantics=("arbitrary", "arbitrary"),
                                             vmem_limit_bytes=VMEM_LIMIT),
        name="mix_ffn",
    )(ret, att, x2, w_out, g_mix_post, g_ffn_pre, w_up, w_up, conv_w, conv_w, conv_b, conv_b, w_down, g_ffn_post)


def _relayout_w_in(w):
    d = w.shape[0]
    w = w.astype(bf16)
    sizes = (GROUP, GROUP, GROUP, GROUP, GROUP, HEAD_DIM, HEAD_DIM, IDX_HEADS * IDX_DIM, IDX_DIM, IDX_HEADS)
    parts, off = [], 0
    for n in sizes:
        parts.append(w[:, off:off + n])
        off += n
    rq, rk, rv, rg, aq, ak, av, iq, ik, iw = parts
    pad = jnp.zeros((d, LANES - IDX_HEADS), w.dtype)
    return jnp.concatenate([rq, rk, aq, iq, rv, rg, ak, ak, av, av, ik, ik, iw, pad], axis=1)


def _rope_tables(seq):
    half = HEAD_DIM // 2
    inv = ROPE_THETA ** (-jnp.arange(0, HEAD_DIM, 2, dtype=f32) / HEAD_DIM)
    ang = jnp.arange(seq, dtype=f32)[:, None] * inv[None, :]
    cos, sin = jnp.cos(ang), jnp.sin(ang)
    cos_t = jnp.tile(cos, (1, LANES // half))
    sin_t = jnp.tile(jnp.concatenate([-sin, sin], axis=1), (1, PAIR))
    return cos_t, sin_t


def kernel(x, mix_norm_pre, mix_norm_post, w_in, w_out, ffn_norm_pre, ffn_norm_post,
           w_up, conv_w, conv_b, w_down):
    batch, seq, d = x.shape
    depth = w_in.shape[0]
    cos_t, sin_t = _rope_tables(seq)
    h = x.reshape(batch * seq, d)
    for l in range(depth):
        rq, rk, aq, iq, rv, rg, kk, vv, ik, iw = _inproj(
            h, mix_norm_pre[l][None, :], _relayout_w_in(w_in[l]), cos_t, sin_t, seq)
        ret = _retention(rq, rk, rv, rg, batch, seq)
        att = _dsa(aq, iq, iw, kk, vv, ik, batch, seq)
        h = _mix_ffn(ret, att, h, w_out[l].astype(bf16), mix_norm_post[l][None, :], ffn_norm_pre[l][None, :],
                     w_up[l].astype(bf16), conv_w[l], conv_b[l][None, :], w_down[l].astype(bf16),
                     ffn_norm_post[l][None, :], batch, seq)
    return h.reshape(batch, seq, d)
```

```python
import functools

import jax
import jax.numpy as jnp
from jax import lax
from jax.experimental import pallas as pl
from jax.experimental.pallas import tpu as pltpu

HEAD_DIM = 64
RET_HEADS = 8
DSA_HEADS = 8
IDX_HEADS = 8
IDX_DIM = 64
TOPK_MAX = 256
D_FF = 2816
CONV_WIDTH = 3
ROPE_THETA = 10000.0
EPS = 1e-6

LANES = 128
SUBLANES = 8
PAIR = LANES // HEAD_DIM
N_PAIRS = RET_HEADS // PAIR
GROUP = RET_HEADS * HEAD_DIM

_OFF_RQ, _OFF_RK, _OFF_AQ, _OFF_IQ, _OFF_RV, _OFF_RG = (i * GROUP for i in range(6))
_OFF_KK = 6 * GROUP
_OFF_VV = _OFF_KK + LANES
_OFF_IK = _OFF_VV + LANES
_OFF_IW = _OFF_IK + LANES
_W_TOTAL = _OFF_IW + LANES

TM_PROJ = 512
RET_CHUNK = 256
TQ = 256
KC = 512
TM_FFN = 512
V7X_VMEM_BYTES = 64 * 1024 * 1024
VMEM_LIMIT = V7X_VMEM_BYTES * 3 // 4

NEG_BIG = -1e30
INT_MIN = -2 ** 31
KEY_NEG_INF = -2139095041
KEY16_NEG_INF = -32641
PACKED_ROWS = 16
N_COUNT_ACC = 4
N_REFINE = 12
SUM_ROWS = 16
LOG2E = 1.4426950408889634

f32 = jnp.float32
bf16 = jnp.bfloat16


def _dot(a, b):
    return jnp.dot(a, b, preferred_element_type=f32)


def _dot_nt(a, b):
    return lax.dot_general(a, b, (((1,), (1,)), ((), ())), preferred_element_type=f32)


def _dot_tn(a, b):
    return lax.dot_general(a, b, (((0,), (0,)), ((), ())), preferred_element_type=f32)


def _rms(x, g):
    return x * lax.rsqrt(jnp.mean(x * x, axis=-1, keepdims=True) + EPS) * g


def _silu(x):
    return x * (1.0 / (1.0 + jnp.exp(-x)))


def _inproj_kernel(x_ref, g_ref, w_ref, ch_ref, sh_ref,
                   rq_ref, rk_ref, aq_ref, iq_ref, rv_ref, rg_ref, kk_ref, vv_ref, ik_ref, iw_ref):
    tm = x_ref.shape[0]
    hn = _rms(x_ref[...], g_ref[...]).astype(bf16)
    cos = ch_ref[...]
    sin = sh_ref[...]
    lane = lax.broadcasted_iota(jnp.int32, (tm, LANES), 1)
    first_half = (lane % HEAD_DIM) < (HEAD_DIM // 2)

    def proj(off, width):
        return _dot(hn, w_ref[:, off:off + width])

    def rope(p):
        swapped = jnp.where(first_half, pltpu.roll(p, LANES - HEAD_DIM // 2, 1),
                            pltpu.roll(p, HEAD_DIM // 2, 1))
        return p * cos + swapped * sin

    def roped_group(off, out_ref, scale):
        p = proj(off, GROUP)
        for b in range(GROUP // LANES):
            r = rope(p[:, b * LANES:(b + 1) * LANES])
            if scale != 1.0:
                r = r * scale
            out_ref[:, b * LANES:(b + 1) * LANES] = r.astype(out_ref.dtype)

    roped_group(_OFF_RQ, rq_ref, 1.0)
    roped_group(_OFF_RK, rk_ref, HEAD_DIM ** -0.5)
    roped_group(_OFF_AQ, aq_ref, HEAD_DIM ** -0.5 * LOG2E)
    roped_group(_OFF_IQ, iq_ref, 1.0)
    rv_ref[...] = proj(_OFF_RV, GROUP).astype(bf16)
    rg_ref[...] = proj(_OFF_RG, GROUP)
    small = proj(_OFF_KK, 4 * LANES)
    kk_ref[...] = rope(small[:, 0:LANES]).astype(bf16)
    vv_ref[...] = small[:, LANES:2 * LANES].astype(bf16)
    ik_ref[...] = rope(small[:, 2 * LANES:3 * LANES]).astype(bf16)
    iw_ref[...] = small[:, 3 * LANES:4 * LANES] * (IDX_HEADS ** -0.5 * IDX_DIM ** -0.5)


def _inproj(x2, g, w, cos_t, sin_t, seq):
    t, d = x2.shape
    tm = min(TM_PROJ, seq)
    n_seq = seq // tm
    row = lambda i: (i, 0)
    fixed = lambda i: (0, 0)
    pos = lambda i: (i % n_seq, 0)
    wide = lambda dt: jax.ShapeDtypeStruct((t, GROUP), dt)
    narrow = lambda dt: jax.ShapeDtypeStruct((t, LANES), dt)
    return pl.pallas_call(
        _inproj_kernel,
        grid=(t // tm,),
        in_specs=[pl.BlockSpec((tm, d), row), pl.BlockSpec((1, d), fixed),
                  pl.BlockSpec((d, _W_TOTAL), fixed),
                  pl.BlockSpec((tm, LANES), pos), pl.BlockSpec((tm, LANES), pos)],
        out_specs=[pl.BlockSpec((tm, GROUP), row)] * 6 + [pl.BlockSpec((tm, LANES), row)] * 4,
        out_shape=[wide(bf16), wide(bf16), wide(bf16), wide(bf16), wide(bf16), wide(f32),
                   narrow(bf16), narrow(bf16), narrow(bf16), narrow(f32)],
        compiler_params=pltpu.CompilerParams(dimension_semantics=("arbitrary",),
                                             vmem_limit_bytes=VMEM_LIMIT),
        name="in_proj",
    )(x2, g, w, cos_t, sin_t)


def _retention_kernel(rq_ref, rk_ref, rv_ref, rg_ref, decay_ref, zeta_ref, xi_ref, cd_ref,
                      out_ref, state_ref):
    c = rq_ref.shape[0]

    @pl.when(pl.program_id(1) == 0)
    def _():
        state_ref[...] = jnp.zeros_like(state_ref)

    lane = lax.broadcasted_iota(jnp.int32, (c, LANES), 1)
    even = lane < HEAD_DIM
    r_i = lax.broadcasted_iota(jnp.int32, (LANES, LANES), 0)
    c_i = lax.broadcasted_iota(jnp.int32, (LANES, LANES), 1)
    same_head = (r_i < HEAD_DIM) == (c_i < HEAD_DIM)
    ones_bd = jnp.where(same_head, 1.0, 0.0).astype(bf16)

    for p in range(N_PAIRS):
        sl = slice(p * LANES, (p + 1) * LANES)
        q2 = rq_ref[:, sl]
        k2 = rk_ref[:, sl]
        v2 = rv_ref[:, sl]
        zero = jnp.zeros_like(q2)
        q_eo = jnp.concatenate([jnp.where(even, q2, zero), jnp.where(even, zero, q2)], axis=0)
        s_eo = _dot_nt(q_eo, k2) * decay_ref[2 * p:2 * p + 2].reshape(2 * c, c)
        pv = _dot(s_eo.astype(bf16), v2)
        inner = jnp.where(even, pv[0:c, :], pv[c:2 * c, :])
        r_prev = state_ref[p]
        cross = _dot(q2, r_prev.astype(bf16)) * xi_ref[p]
        o = inner + cross
        kz = (k2.astype(f32) * zeta_ref[p]).astype(bf16)
        s_new = jnp.where(same_head, _dot_tn(kz, v2), 0.0)
        state_ref[p] = cd_ref[p] * r_prev + s_new
        sq = o * o
        hi = sq.astype(bf16)
        lo = (sq - hi.astype(f32)).astype(bf16)
        sums = _dot(jnp.concatenate([hi, lo], axis=0), ones_bd)
        ms = (sums[0:c, :] + sums[c:2 * c, :]) * (1.0 / HEAD_DIM)
        o = o * lax.rsqrt(ms + EPS)
        out_ref[:, sl] = (o * _silu(rg_ref[:, sl])).astype(out_ref.dtype)


def _retention_tables(c):
    h = jnp.arange(RET_HEADS, dtype=f32)
    log_g = jnp.log(1.0 - 2.0 ** (-5.0 - h))
    pos = jnp.arange(c, dtype=f32)
    diff = pos[:, None] - pos[None, :]
    decay = jnp.where(diff >= 0, jnp.exp(log_g[:, None, None] * jnp.maximum(diff, 0.0)), 0.0)
    zeta = jnp.exp(log_g[:, None] * (c - 1.0 - pos)[None, :])
    xi = jnp.exp(log_g[:, None] * (pos + 1.0)[None, :])
    cd = jnp.exp(log_g * c)

    def pair_lanes(a):
        a = a.reshape(N_PAIRS, PAIR, c)
        return jnp.repeat(jnp.transpose(a, (0, 2, 1)), HEAD_DIM, axis=2)

    cd2 = jnp.repeat(cd.reshape(N_PAIRS, 1, PAIR), HEAD_DIM, axis=2)
    return decay, pair_lanes(zeta), pair_lanes(xi), cd2


def _retention(rq, rk, rv, rg, batch, seq):
    t = rq.shape[0]
    c = min(RET_CHUNK, seq)
    n = seq // c
    decay, zeta, xi, cd = _retention_tables(c)
    row = lambda b, i: (b * n + i, 0)
    fix3 = lambda b, i: (0, 0, 0)
    return pl.pallas_call(
        _retention_kernel,
        grid=(batch, n),
        in_specs=[pl.BlockSpec((c, GROUP), row)] * 4 + [
            pl.BlockSpec((RET_HEADS, c, c), fix3), pl.BlockSpec((N_PAIRS, c, LANES), fix3),
            pl.BlockSpec((N_PAIRS, c, LANES), fix3), pl.BlockSpec((N_PAIRS, 1, LANES), fix3)],
        out_specs=pl.BlockSpec((c, GROUP), row),
        out_shape=jax.ShapeDtypeStruct((t, GROUP), bf16),
        scratch_shapes=[pltpu.VMEM((N_PAIRS, LANES, LANES), f32)],
        compiler_params=pltpu.CompilerParams(dimension_semantics=("arbitrary", "arbitrary"),
                                             vmem_limit_bytes=VMEM_LIMIT),
        name="retention",
    )(rq, rk, rv, rg, decay, zeta, xi, cd)


def _key_to_float(key):
    bits = jnp.where(key >= 0, key, key ^ jnp.int32(0x7FFFFFFF))
    return lax.bitcast_convert_type(bits, f32)


def _dsa_kernel(k_sel, aq_ref, iq_ref, iw_ref, kk_ref, vv_ref, ik_ref, out_ref,
                lhs_a, lhs_i, xs, xb, bias, prod, p_ref, m_ref, alpha_ref, acc_ref, vaug, earlier):
    tq = aq_ref.shape[0]
    n_kc, kc, _ = xs.shape
    j = pl.program_id(1)
    half = kc // 2
    n_keys = (j + 1) * tq
    n_full = n_keys // kc
    has_half = n_keys % kc != 0
    kf = float(k_sel)

    def over_chunks(body, init):
        carry = lax.fori_loop(0, n_full, lambda c, a: body(c, kc, a), init)
        return lax.cond(has_half, lambda a: body(n_full, half, a), lambda a: a, carry)

    @pl.when((pl.program_id(0) == 0) & (j == 0))
    def _():
        r_i = lax.broadcasted_iota(jnp.int32, (kc, kc), 0)
        c_i = lax.broadcasted_iota(jnp.int32, (kc, kc), 1)
        earlier[...] = jnp.where(c_i < r_i, 1.0, 0.0).astype(bf16)

    @pl.when(j == 0)
    def _():
        row = lax.broadcasted_iota(jnp.int32, (LANES, kc), 0)
        for c in range(n_kc):
            vt = vv_ref[c * kc:(c + 1) * kc, :].astype(f32).T
            vaug[c] = jnp.where(row < HEAD_DIM, vt, 1.0)[0:HEAD_DIM + SUM_ROWS, :].astype(bf16)

    lane = lax.broadcasted_iota(jnp.int32, (tq, LANES), 1)
    even = lane < HEAD_DIM
    for p in range(N_PAIRS):
        sl = slice(p * LANES, (p + 1) * LANES)
        a2 = aq_ref[:, sl]
        i2 = iq_ref[:, sl]
        zero = jnp.zeros_like(a2)
        lhs_a[(2 * p) * tq:(2 * p + 1) * tq, :] = jnp.where(even, a2, zero)
        lhs_a[(2 * p + 1) * tq:(2 * p + 2) * tq, :] = jnp.where(even, zero, a2)
        lhs_i[(2 * p) * tq:(2 * p + 1) * tq, :] = jnp.where(even, i2, zero)
        lhs_i[(2 * p + 1) * tq:(2 * p + 2) * tq, :] = jnp.where(even, zero, i2)
    w_heads = iw_ref[...].T[0:IDX_HEADS, :]

    def for_key_chunks(key_ref, lhs_ref, work):
        def one_chunk(c, rows, _):
            start = pl.multiple_of(c * kc, half)
            prod[0:rows, :] = _dot_nt(key_ref[pl.ds(start, rows), :], lhs_ref[...])
            work(c, rows)
            return 0

        over_chunks(one_chunk, 0)

    def score_chunk(c, rows):
        isc = jnp.zeros((rows, tq), f32)
        for h in range(IDX_HEADS):
            isc = isc + w_heads[h:h + 1, :] * jnp.maximum(prod[0:rows, h * tq:(h + 1) * tq], 0.0)
        key_pos = c * kc + lax.broadcasted_iota(jnp.int32, (rows, tq), 0)
        q_pos = n_keys - tq + lax.broadcasted_iota(jnp.int32, (rows, tq), 1)
        isc = jnp.where(key_pos <= q_pos, isc, -jnp.inf)
        xs[c, 0:rows, :] = isc
        xb[c, 0:rows, :] = isc.astype(xb.dtype)

    for_key_chunks(ik_ref, lhs_i, score_chunk)

    def count_where(pred):
        def body(c, rows, accs):
            x = xs[c, 0:rows, :]
            accs = list(accs)
            for r in range(rows // SUBLANES):
                hit = jnp.where(pred(x[r * SUBLANES:(r + 1) * SUBLANES, :]), 1.0, 0.0)
                accs[r % N_COUNT_ACC] = accs[r % N_COUNT_ACC] + hit
            return tuple(accs)
        zero = jnp.zeros((SUBLANES, tq), f32)
        accs = over_chunks(body, (zero,) * N_COUNT_ACC)
        total = functools.reduce(lambda a, b: a + b, accs)
        return jnp.broadcast_to(jnp.sum(total, axis=0, keepdims=True), (SUBLANES, tq))

    def count_ge(cand_f):
        return count_where(lambda x: x >= cand_f)

    def count_ge_packed(cand_b):
        one = jnp.ones((PACKED_ROWS, tq), xb.dtype)
        zero = jnp.zeros((PACKED_ROWS, tq), xb.dtype)

        def body(c, rows, accs):
            x = xb[c, 0:rows, :]
            accs = list(accs)
            for r in range(rows // PACKED_ROWS):
                hit = jnp.where(x[r * PACKED_ROWS:(r + 1) * PACKED_ROWS, :] >= cand_b, one, zero)
                accs[r % N_COUNT_ACC] = accs[r % N_COUNT_ACC] + hit
            return tuple(accs)
        accs = over_chunks(body, (zero,) * N_COUNT_ACC)
        total = functools.reduce(lambda a, b: a + b, [a.astype(f32) for a in accs])
        return jnp.broadcast_to(jnp.sum(total, axis=0, keepdims=True), (SUBLANES, tq))

    def key16_to_f32_bits(key16):
        b16 = jnp.where(key16 >= 0, key16, key16 ^ jnp.int32(0x7FFF)) & jnp.int32(0xFFFF)
        return lax.shift_left(b16, jnp.int32(16))

    def bit_step_packed(i, t16):
        cand = t16 + lax.shift_left(jnp.int32(1), jnp.int32(15) - i)
        cand_f = lax.bitcast_convert_type(key16_to_f32_bits(cand), f32)
        cand_b = jnp.broadcast_to(cand_f[0:1, :].astype(xb.dtype), (PACKED_ROWS, tq))
        return jnp.where(count_ge_packed(cand_b) >= kf, cand, t16)

    t16 = lax.fori_loop(0, 16, bit_step_packed, jnp.full((SUBLANES, tq), -2 ** 15, jnp.int32))

    c0_bits = key16_to_f32_bits(t16)
    c0_key = jnp.where(c0_bits >= 0, c0_bits, c0_bits ^ jnp.int32(0x7FFFFFFF))
    has_k = t16 > KEY16_NEG_INF

    def bit_step(i, carry):
        t_key, t_cnt = carry
        cand = t_key + lax.shift_left(jnp.int32(1), jnp.int32(16) - i)
        cnt = count_ge(_key_to_float(cand))
        ok = cnt >= kf
        return jnp.where(ok, cand, t_key), jnp.where(ok, cnt, t_cnt)

    t_key, t_cnt = lax.fori_loop(0, 17, bit_step, (c0_key - 2 ** 16, jnp.full((SUBLANES, tq), kf, f32)))
    t_key = jnp.where(has_k, t_key, INT_MIN)

    t_sel = _key_to_float(jnp.maximum(t_key, KEY_NEG_INF + 1))[0:1, :]

    def bias_chunk(c, rows, _):
        bias[c, 0:rows, :] = jnp.where(xs[c, 0:rows, :] >= t_sel, 0.0, NEG_BIG).astype(bias.dtype)
        return 0

    over_chunks(bias_chunk, 0)

    real = t_key > KEY_NEG_INF
    tied = (t_cnt > kf) & real

    @pl.when(jnp.max(jnp.where(tied, 1.0, 0.0)) > 0.0)
    def _():
        lo0 = _key_to_float(t_key)
        hi0 = _key_to_float(t_key + 1)
        n_hi0 = count_ge(hi0)
        n_eq = count_where(lambda x: x == lo0)
        unequal = tied & (n_eq != t_cnt - n_hi0)

        def refined(args):
            def refine(_, carry):
                lo, hi, n_hi = carry
                mid = lo + (hi - lo) * 0.5
                cnt = count_ge(mid)
                ok = cnt >= kf
                return jnp.where(ok, mid, lo), jnp.where(ok, hi, mid), jnp.where(ok, n_hi, cnt)
            return lax.fori_loop(0, N_REFINE, refine, args)

        lo8, hi8, n_hi8 = lax.cond(jnp.max(jnp.where(unequal, 1.0, 0.0)) > 0.0, refined, lambda args: args,
                                   (lo0, hi0, n_hi0))
        need8 = kf - n_hi8
        t_sel8 = _key_to_float(jnp.maximum(t_key, KEY_NEG_INF + 1))

        def tie_chunk(c, rows, seen8):
            slabs = (rows // SUBLANES, SUBLANES, tq)
            x = xs[c, 0:rows, :].reshape(slabs)
            above = jnp.where(x >= hi8, 1.0, 0.0)
            tie = jnp.where(x >= lo8, 1.0, 0.0) - above
            before = _dot(earlier[0:rows, 0:rows], tie.reshape(rows, tq).astype(bf16)).reshape(slabs)
            sel = above + jnp.where(before + seen8 < need8, tie, 0.0)
            sel = jnp.where(real, sel, jnp.where(x >= t_sel8, 1.0, 0.0))
            bias[c, 0:rows, :] = ((sel - 1.0) * -NEG_BIG).reshape(rows, tq).astype(bias.dtype)
            total = jnp.sum(jnp.sum(tie, axis=0), axis=0, keepdims=True)
            return seen8 + jnp.broadcast_to(total, (SUBLANES, tq))

        over_chunks(tie_chunk, jnp.zeros((SUBLANES, tq), f32))

    m_ref[...] = jnp.full(m_ref.shape, -jnp.inf, f32)
    acc_ref[...] = jnp.zeros(acc_ref.shape, f32)

    def attend_chunk(c, rows):
        b = bias[c, 0:rows, :]
        for h in range(DSA_HEADS):
            hs = slice(h * tq, (h + 1) * tq)
            s = prod[0:rows, hs].astype(bf16) + b
            m_old = m_ref[:, hs]
            m_new = jnp.maximum(m_old, jnp.max(s, axis=0, keepdims=True).astype(f32))
            alpha_ref[:, hs] = jnp.exp2(m_old - m_new)
            p_ref[0:rows, hs] = jnp.exp2(s - m_new.astype(bf16))
            m_ref[:, hs] = m_new
        acc_ref[...] = alpha_ref[...] * acc_ref[...] + _dot(vaug[c, :, 0:rows], p_ref[0:rows, :])

    for_key_chunks(kk_ref, lhs_a, attend_chunk)

    for p in range(N_PAIRS):
        halves = []
        for e in range(PAIR):
            hs = slice((PAIR * p + e) * tq, (PAIR * p + e + 1) * tq)
            halves.append(acc_ref[0:HEAD_DIM, hs] * (1.0 / acc_ref[HEAD_DIM:HEAD_DIM + 1, hs]))
        out_ref[:, p * LANES:(p + 1) * LANES] = jnp.concatenate(halves, axis=0).T.astype(out_ref.dtype)


def _dsa(aq, iq, iw, kk, vv, ik, batch, seq):
    t = aq.shape[0]
    tq = min(TQ, seq)
    kc = min(KC, seq)
    assert kc in (tq, 2 * tq), "a query tile must end on a whole or a half key chunk"
    nq = seq // tq
    k_sel = min(TOPK_MAX, seq // 4)
    heads = DSA_HEADS
    qrow = lambda b, j: (b * nq + j, 0)
    brow = lambda b, j: (b, 0)
    return pl.pallas_call(
        functools.partial(_dsa_kernel, k_sel),
        grid=(batch, nq),
        in_specs=[pl.BlockSpec((tq, GROUP), qrow), pl.BlockSpec((tq, GROUP), qrow),
                  pl.BlockSpec((tq, LANES), qrow),
                  pl.BlockSpec((seq, LANES), brow), pl.BlockSpec((seq, LANES), brow),
                  pl.BlockSpec((seq, LANES), brow)],
        out_specs=pl.BlockSpec((tq, GROUP), qrow),
        out_shape=jax.ShapeDtypeStruct((t, GROUP), bf16),
        scratch_shapes=[
            pltpu.VMEM((heads * tq, LANES), bf16),
            pltpu.VMEM((heads * tq, LANES), bf16),
            pltpu.VMEM((seq // kc, kc, tq), f32),
            pltpu.VMEM((seq // kc, kc, tq), bf16),
            pltpu.VMEM((seq // kc, kc, tq), bf16),
            pltpu.VMEM((kc, heads * tq), f32),
            pltpu.VMEM((kc, heads * tq), bf16),
            pltpu.VMEM((1, heads * tq), f32),
            pltpu.VMEM((1, heads * tq), f32),
            pltpu.VMEM((HEAD_DIM + SUM_ROWS, heads * tq), f32),
            pltpu.VMEM((seq // kc, HEAD_DIM + SUM_ROWS, kc), bf16),
            pltpu.VMEM((kc, kc), bf16),
        ],
        compiler_params=pltpu.CompilerParams(dimension_semantics=("arbitrary", "arbitrary"),
                                             vmem_limit_bytes=VMEM_LIMIT),
        name="dsa",
    )(aq, iq, iw, kk, vv, ik)


def _mix_ffn_kernel(ret_ref, att_ref, x_ref, wo_ref, gmix_ref, gpre_ref, wg_ref, wu_ref, cwg_ref, cwu_ref,
                    cbg_ref, cbu_ref, wd_ref, gpost_ref, out_ref, carry_ref):
    tm = x_ref.shape[0]

    @pl.when(pl.program_id(1) == 0)
    def _():
        carry_ref[...] = jnp.zeros_like(carry_ref)

    mixed = _dot(ret_ref[...], wo_ref[0:GROUP, :]) + _dot(att_ref[...], wo_ref[GROUP:2 * GROUP, :])
    h = x_ref[...] + _rms(mixed, gmix_ref[...])

    hn = _rms(h, gpre_ref[...]).astype(bf16)
    row = lax.broadcasted_iota(jnp.int32, (tm, wg_ref.shape[1]), 0)

    def conv(part, w_ref, b_ref, up):
        tail = carry_ref[part]
        carry_ref[part] = up[tm - SUBLANES:tm, :]
        m1 = jnp.where(row == 0, tail[7:8, :], pltpu.roll(up, 1, 0))
        m2 = jnp.where(row == 0, tail[6:7, :], jnp.where(row == 1, tail[7:8, :], pltpu.roll(up, 2, 0)))
        return b_ref[...] + m2 * w_ref[0:1, :] + m1 * w_ref[1:2, :] + up * w_ref[2:3, :]

    g = conv(0, cwg_ref, cbg_ref, _dot(hn, wg_ref[...]))
    u = conv(1, cwu_ref, cbu_ref, _dot(hn, wu_ref[...]))
    ffn = _dot((_silu(g) * u).astype(bf16), wd_ref[...])
    out_ref[...] = h + _rms(ffn, gpost_ref[...])


def _mix_ffn(ret, att, x2, w_out, g_mix_post, g_ffn_pre, w_up, conv_w, conv_b, w_down, g_ffn_post, batch, seq):
    t, d = x2.shape
    tm = min(TM_FFN, seq)
    n_seq = seq // tm
    once = pl.Buffered(1)
    row = lambda b, i: (b * n_seq + i, 0)
    fixed = lambda b, i: (0, 0)
    second = lambda b, i: (0, 1)
    return pl.pallas_call(
        _mix_ffn_kernel,
        grid=(batch, n_seq),
        in_specs=[pl.BlockSpec((tm, GROUP), row), pl.BlockSpec((tm, GROUP), row), pl.BlockSpec((tm, d), row),
                  pl.BlockSpec((2 * GROUP, d), fixed, pipeline_mode=once),
                  pl.BlockSpec((1, d), fixed), pl.BlockSpec((1, d), fixed),
                  pl.BlockSpec((d, D_FF), fixed, pipeline_mode=once),
                  pl.BlockSpec((d, D_FF), second, pipeline_mode=once),
                  pl.BlockSpec((CONV_WIDTH, D_FF), fixed), pl.BlockSpec((CONV_WIDTH, D_FF), second),
                  pl.BlockSpec((1, D_FF), fixed), pl.BlockSpec((1, D_FF), second),
                  pl.BlockSpec((D_FF, d), fixed, pipeline_mode=once),
                  pl.BlockSpec((1, d), fixed)],
        out_specs=pl.BlockSpec((tm, d), row),
        out_shape=jax.ShapeDtypeStruct((t, d), f32),
        scratch_shapes=[pltpu.VMEM((2, SUBLANES, D_FF), f32)],
        compiler_params=pltpu.CompilerParams(dimension_semantics=("arbitrary", "arbitrary"),
                                             vmem_limit_bytes=VMEM_LIMIT),
        name="mix_ffn",
    )(ret, att, x2, w_out, g_mix_post, g_ffn_pre, w_up, w_up, conv_w, conv_w, conv_b, conv_b, w_down, g_ffn_post)


def _relayout_w_in(w):
    d = w.shape[0]
    w = w.astype(bf16)
    sizes = (GROUP, GROUP, GROUP, GROUP, GROUP, HEAD_DIM, HEAD_DIM, IDX_HEADS * IDX_DIM, IDX_DIM, IDX_HEADS)
    parts, off = [], 0
    for n in sizes:
        parts.append(w[:, off:off + n])
        off += n
    rq, rk, rv, rg, aq, ak, av, iq, ik, iw = parts
    pad = jnp.zeros((d, LANES - IDX_HEADS), w.dtype)
    return jnp.concatenate([rq, rk, aq, iq, rv, rg, ak, ak, av, av, ik, ik, iw, pad], axis=1)


def _rope_tables(seq):
    half = HEAD_DIM // 2
    inv = ROPE_THETA ** (-jnp.arange(0, HEAD_DIM, 2, dtype=f32) / HEAD_DIM)
    ang = jnp.arange(seq, dtype=f32)[:, None] * inv[None, :]
    cos, sin = jnp.cos(ang), jnp.sin(ang)
    cos_t = jnp.tile(cos, (1, LANES // half))
    sin_t = jnp.tile(jnp.concatenate([-sin, sin], axis=1), (1, PAIR))
    return cos_t, sin_t


def kernel(x, mix_norm_pre, mix_norm_post, w_in, w_out, ffn_norm_pre, ffn_norm_post,
           w_up, conv_w, conv_b, w_down):
    batch, seq, d = x.shape
    depth = w_in.shape[0]
    cos_t, sin_t = _rope_tables(seq)
    h = x.reshape(batch * seq, d)
    for l in range(depth):
        rq, rk, aq, iq, rv, rg, kk, vv, ik, iw = _inproj(
            h, mix_norm_pre[l][None, :], _relayout_w_in(w_in[l]), cos_t, sin_t, seq)
        ret = _retention(rq, rk, rv, rg, batch, seq)
        att = _dsa(aq, iq, iw, kk, vv, ik, batch, seq)
        h = _mix_ffn(ret, att, h, w_out[l].astype(bf16), mix_norm_post[l][None, :], ffn_norm_pre[l][None, :],
                     w_up[l].astype(bf16), conv_w[l], conv_b[l][None, :], w_down[l].astype(bf16),
                     ffn_norm_post[l][None, :], batch, seq)
    return h.reshape(batch, seq, d)
```

```python
import functools

import jax
import jax.numpy as jnp
from jax import lax
from jax.experimental import pallas as pl
from jax.experimental.pallas import tpu as pltpu

HEAD_DIM = 64
RET_HEADS = 8
DSA_HEADS = 8
IDX_HEADS = 8
IDX_DIM = 64
TOPK_MAX = 256
D_FF = 2816
CONV_WIDTH = 3
ROPE_THETA = 10000.0
EPS = 1e-6

LANES = 128
SUBLANES = 8
PAIR = LANES // HEAD_DIM
N_PAIRS = RET_HEADS // PAIR
GROUP = RET_HEADS * HEAD_DIM

_OFF_RQ, _OFF_RK, _OFF_AQ, _OFF_IQ, _OFF_RV, _OFF_RG = (i * GROUP for i in range(6))
_OFF_KK = 6 * GROUP
_OFF_VV = _OFF_KK + LANES
_OFF_IK = _OFF_VV + LANES
_OFF_IW = _OFF_IK + LANES
_W_TOTAL = _OFF_IW + LANES

TM_PROJ = 512
RET_CHUNK = 256
TQ = 256
KC = 512
TM_FFN = 512
V7X_VMEM_BYTES = 64 * 1024 * 1024
VMEM_LIMIT = V7X_VMEM_BYTES * 3 // 4

NEG_BIG = -1e30
INT_MIN = -2 ** 31
KEY_NEG_INF = -2139095041
KEY16_NEG_INF = -32641
PACKED_ROWS = 16
N_COUNT_ACC = 4
N_REFINE = 12
SUM_ROWS = 16
LOG2E = 1.4426950408889634

f32 = jnp.float32
bf16 = jnp.bfloat16


def _dot(a, b):
    return jnp.dot(a, b, preferred_element_type=f32)


def _dot_nt(a, b):
    return lax.dot_general(a, b, (((1,), (1,)), ((), ())), preferred_element_type=f32)


def _dot_tn(a, b):
    return lax.dot_general(a, b, (((0,), (0,)), ((), ())), preferred_element_type=f32)


def _rms(x, g):
    return x * lax.rsqrt(jnp.mean(x * x, axis=-1, keepdims=True) + EPS) * g


def _silu(x):
    return x * (1.0 / (1.0 + jnp.exp(-x)))


def _inproj_kernel(x_ref, g_ref, w_ref, ch_ref, sh_ref,
                   rq_ref, rk_ref, aq_ref, iq_ref, rv_ref, rg_ref, kk_ref, vv_ref, ik_ref, iw_ref):
    tm = x_ref.shape[0]
    hn = _rms(x_ref[...], g_ref[...]).astype(bf16)
    cos = ch_ref[...]
    sin = sh_ref[...]
    lane = lax.broadcasted_iota(jnp.int32, (tm, LANES), 1)
    first_half = (lane % HEAD_DIM) < (HEAD_DIM // 2)

    def proj(off, width):
        return _dot(hn, w_ref[:, off:off + width])

    def rope(p):
        swapped = jnp.where(first_half, pltpu.roll(p, LANES - HEAD_DIM // 2, 1),
                            pltpu.roll(p, HEAD_DIM // 2, 1))
        return p * cos + swapped * sin

    def roped_group(off, out_ref, scale):
        p = proj(off, GROUP)
        for b in range(GROUP // LANES):
            r = rope(p[:, b * LANES:(b + 1) * LANES])
            if scale != 1.0:
                r = r * scale
            out_ref[:, b * LANES:(b + 1) * LANES] = r.astype(out_ref.dtype)

    roped_group(_OFF_RQ, rq_ref, 1.0)
    roped_group(_OFF_RK, rk_ref, HEAD_DIM ** -0.5)
    roped_group(_OFF_AQ, aq_ref, HEAD_DIM ** -0.5 * LOG2E)
    roped_group(_OFF_IQ, iq_ref, 1.0)
    rv_ref[...] = proj(_OFF_RV, GROUP).astype(bf16)
    rg_ref[...] = proj(_OFF_RG, GROUP)
    small = proj(_OFF_KK, 4 * LANES)
    kk_ref[...] = rope(small[:, 0:LANES]).astype(bf16)
    vv_ref[...] = small[:, LANES:2 * LANES].astype(bf16)
    ik_ref[...] = rope(small[:, 2 * LANES:3 * LANES]).astype(bf16)
    iw_ref[...] = small[:, 3 * LANES:4 * LANES] * (IDX_HEADS ** -0.5 * IDX_DIM ** -0.5)


def _inproj(x2, g, w, cos_t, sin_t, seq):
    t, d = x2.shape
    tm = min(TM_PROJ, seq)
    n_seq = seq // tm
    row = lambda i: (i, 0)
    fixed = lambda i: (0, 0)
    pos = lambda i: (i % n_seq, 0)
    wide = lambda dt: jax.ShapeDtypeStruct((t, GROUP), dt)
    narrow = lambda dt: jax.ShapeDtypeStruct((t, LANES), dt)
    return pl.pallas_call(
        _inproj_kernel,
        grid=(t // tm,),
        in_specs=[pl.BlockSpec((tm, d), row), pl.BlockSpec((1, d), fixed),
                  pl.BlockSpec((d, _W_TOTAL), fixed),
                  pl.BlockSpec((tm, LANES), pos), pl.BlockSpec((tm, LANES), pos)],
        out_specs=[pl.BlockSpec((tm, GROUP), row)] * 6 + [pl.BlockSpec((tm, LANES), row)] * 4,
        out_shape=[wide(bf16), wide(bf16), wide(bf16), wide(bf16), wide(bf16), wide(f32),
                   narrow(bf16), narrow(bf16), narrow(bf16), narrow(f32)],
        compiler_params=pltpu.CompilerParams(dimension_semantics=("arbitrary",),
                                             vmem_limit_bytes=VMEM_LIMIT),
        name="in_proj",
    )(x2, g, w, cos_t, sin_t)


def _retention_kernel(rq_ref, rk_ref, rv_ref, rg_ref, decay_ref, zeta_ref, xi_ref, cd_ref,
                      out_ref, state_ref):
    c = rq_ref.shape[0]

    @pl.when(pl.program_id(1) == 0)
    def _():
        state_ref[...] = jnp.zeros_like(state_ref)

    lane = lax.broadcasted_iota(jnp.int32, (c, LANES), 1)
    even = lane < HEAD_DIM
    r_i = lax.broadcasted_iota(jnp.int32, (LANES, LANES), 0)
    c_i = lax.broadcasted_iota(jnp.int32, (LANES, LANES), 1)
    same_head = (r_i < HEAD_DIM) == (c_i < HEAD_DIM)
    ones_bd = jnp.where(same_head, 1.0, 0.0).astype(bf16)

    for p in range(N_PAIRS):
        sl = slice(p * LANES, (p + 1) * LANES)
        q2 = rq_ref[:, sl]
        k2 = rk_ref[:, sl]
        v2 = rv_ref[:, sl]
        zero = jnp.zeros_like(q2)
        q_eo = jnp.concatenate([jnp.where(even, q2, zero), jnp.where(even, zero, q2)], axis=0)
        s_eo = _dot_nt(q_eo, k2) * decay_ref[2 * p:2 * p + 2].reshape(2 * c, c)
        pv = _dot(s_eo.astype(bf16), v2)
        inner = jnp.where(even, pv[0:c, :], pv[c:2 * c, :])
        r_prev = state_ref[p]
        cross = _dot(q2, r_prev.astype(bf16)) * xi_ref[p]
        o = inner + cross
        kz = (k2.astype(f32) * zeta_ref[p]).astype(bf16)
        s_new = jnp.where(same_head, _dot_tn(kz, v2), 0.0)
        state_ref[p] = cd_ref[p] * r_prev + s_new
        sq = o * o
        hi = sq.astype(bf16)
        lo = (sq - hi.astype(f32)).astype(bf16)
        sums = _dot(jnp.concatenate([hi, lo], axis=0), ones_bd)
        ms = (sums[0:c, :] + sums[c:2 * c, :]) * (1.0 / HEAD_DIM)
        o = o * lax.rsqrt(ms + EPS)
        out_ref[:, sl] = (o * _silu(rg_ref[:, sl])).astype(out_ref.dtype)


def _retention_tables(c):
    h = jnp.arange(RET_HEADS, dtype=f32)
    log_g = jnp.log(1.0 - 2.0 ** (-5.0 - h))
    pos = jnp.arange(c, dtype=f32)
    diff = pos[:, None] - pos[None, :]
    decay = jnp.where(diff >= 0, jnp.exp(log_g[:, None, None] * jnp.maximum(diff, 0.0)), 0.0)
    zeta = jnp.exp(log_g[:, None] * (c - 1.0 - pos)[None, :])
    xi = jnp.exp(log_g[:, None] * (pos + 1.0)[None, :])
    cd = jnp.exp(log_g * c)

    def pair_lanes(a):
        a = a.reshape(N_PAIRS, PAIR, c)
        return jnp.repeat(jnp.transpose(a, (0, 2, 1)), HEAD_DIM, axis=2)

    cd2 = jnp.repeat(cd.reshape(N_PAIRS, 1, PAIR), HEAD_DIM, axis=2)
    return decay, pair_lanes(zeta), pair_lanes(xi), cd2


def _retention(rq, rk, rv, rg, batch, seq):
    t = rq.shape[0]
    c = min(RET_CHUNK, seq)
    n = seq // c
    decay, zeta, xi, cd = _retention_tables(c)
    row = lambda b, i: (b * n + i, 0)
    fix3 = lambda b, i: (0, 0, 0)
    return pl.pallas_call(
        _retention_kernel,
        grid=(batch, n),
        in_specs=[pl.BlockSpec((c, GROUP), row)] * 4 + [
            pl.BlockSpec((RET_HEADS, c, c), fix3), pl.BlockSpec((N_PAIRS, c, LANES), fix3),
            pl.BlockSpec((N_PAIRS, c, LANES), fix3), pl.BlockSpec((N_PAIRS, 1, LANES), fix3)],
        out_specs=pl.BlockSpec((c, GROUP), row),
        out_shape=jax.ShapeDtypeStruct((t, GROUP), bf16),
        scratch_shapes=[pltpu.VMEM((N_PAIRS, LANES, LANES), f32)],
        compiler_params=pltpu.CompilerParams(dimension_semantics=("arbitrary", "arbitrary"),
                                             vmem_limit_bytes=VMEM_LIMIT),
        name="retention",
    )(rq, rk, rv, rg, decay, zeta, xi, cd)


def _key_to_float(key):
    bits = jnp.where(key >= 0, key, key ^ jnp.int32(0x7FFFFFFF))
    return lax.bitcast_convert_type(bits, f32)


def _dsa_kernel(k_sel, aq_ref, iq_ref, iw_ref, kk_ref, vv_ref, ik_ref, out_ref,
                lhs_a, lhs_i, xs, xb, bias, prod, logits, p_ref, m_ref, alpha_ref, acc_ref, vaug, earlier):
    tq = aq_ref.shape[0]
    n_kc, kc, _ = xs.shape
    j = pl.program_id(1)
    half = kc // 2
    n_keys = (j + 1) * tq
    n_full = n_keys // kc
    has_half = n_keys % kc != 0
    kf = float(k_sel)

    def over_chunks(body, init):
        carry = lax.fori_loop(0, n_full, lambda c, a: body(c, kc, a), init)
        return lax.cond(has_half, lambda a: body(n_full, half, a), lambda a: a, carry)

    @pl.when((pl.program_id(0) == 0) & (j == 0))
    def _():
        r_i = lax.broadcasted_iota(jnp.int32, (kc, kc), 0)
        c_i = lax.broadcasted_iota(jnp.int32, (kc, kc), 1)
        earlier[...] = jnp.where(c_i < r_i, 1.0, 0.0).astype(bf16)

    @pl.when(j == 0)
    def _():
        row = lax.broadcasted_iota(jnp.int32, (LANES, kc), 0)
        for c in range(n_kc):
            vt = vv_ref[c * kc:(c + 1) * kc, :].astype(f32).T
            vaug[c] = jnp.where(row < HEAD_DIM, vt, 1.0)[0:HEAD_DIM + SUM_ROWS, :].astype(bf16)

    lane = lax.broadcasted_iota(jnp.int32, (tq, LANES), 1)
    even = lane < HEAD_DIM
    for p in range(N_PAIRS):
        sl = slice(p * LANES, (p + 1) * LANES)
        a2 = aq_ref[:, sl]
        i2 = iq_ref[:, sl]
        zero = jnp.zeros_like(a2)
        lhs_a[(2 * p) * tq:(2 * p + 1) * tq, :] = jnp.where(even, a2, zero)
        lhs_a[(2 * p + 1) * tq:(2 * p + 2) * tq, :] = jnp.where(even, zero, a2)
        lhs_i[(2 * p) * tq:(2 * p + 1) * tq, :] = jnp.where(even, i2, zero)
        lhs_i[(2 * p + 1) * tq:(2 * p + 2) * tq, :] = jnp.where(even, zero, i2)
    w_heads = iw_ref[...].T[0:IDX_HEADS, :]

    def score_chunk(c, rows, _):
        start = pl.multiple_of(c * kc, half)
        prod[0:rows, :] = _dot_nt(ik_ref[pl.ds(start, rows), :], lhs_i[...])
        logits[c, 0:rows, :] = _dot_nt(kk_ref[pl.ds(start, rows), :], lhs_a[...]).astype(bf16)
        isc = jnp.zeros((rows, tq), f32)
        for h in range(IDX_HEADS):
            isc = isc + w_heads[h:h + 1, :] * jnp.maximum(prod[0:rows, h * tq:(h + 1) * tq], 0.0)
        key_pos = c * kc + lax.broadcasted_iota(jnp.int32, (rows, tq), 0)
        q_pos = n_keys - tq + lax.broadcasted_iota(jnp.int32, (rows, tq), 1)
        isc = jnp.where(key_pos <= q_pos, isc, -jnp.inf)
        xs[c, 0:rows, :] = isc
        xb[c, 0:rows, :] = isc.astype(xb.dtype)
        return 0

    over_chunks(score_chunk, 0)

    def count_where(pred):
        def body(c, rows, accs):
            x = xs[c, 0:rows, :]
            accs = list(accs)
            for r in range(rows // SUBLANES):
                hit = jnp.where(pred(x[r * SUBLANES:(r + 1) * SUBLANES, :]), 1.0, 0.0)
                accs[r % N_COUNT_ACC] = accs[r % N_COUNT_ACC] + hit
            return tuple(accs)
        zero = jnp.zeros((SUBLANES, tq), f32)
        accs = over_chunks(body, (zero,) * N_COUNT_ACC)
        total = functools.reduce(lambda a, b: a + b, accs)
        return jnp.broadcast_to(jnp.sum(total, axis=0, keepdims=True), (SUBLANES, tq))

    def count_ge(cand_f):
        return count_where(lambda x: x >= cand_f)

    def count_ge_packed(cand_b):
        one = jnp.ones((PACKED_ROWS, tq), xb.dtype)
        zero = jnp.zeros((PACKED_ROWS, tq), xb.dtype)

        def body(c, rows, accs):
            x = xb[c, 0:rows, :]
            accs = list(accs)
            for r in range(rows // PACKED_ROWS):
                hit = jnp.where(x[r * PACKED_ROWS:(r + 1) * PACKED_ROWS, :] >= cand_b, one, zero)
                accs[r % N_COUNT_ACC] = accs[r % N_COUNT_ACC] + hit
            return tuple(accs)
        accs = over_chunks(body, (zero,) * N_COUNT_ACC)
        total = functools.reduce(lambda a, b: a + b, [a.astype(f32) for a in accs])
        return jnp.broadcast_to(jnp.sum(total, axis=0, keepdims=True), (SUBLANES, tq))

    def key16_to_f32_bits(key16):
        b16 = jnp.where(key16 >= 0, key16, key16 ^ jnp.int32(0x7FFF)) & jnp.int32(0xFFFF)
        return lax.shift_left(b16, jnp.int32(16))

    def bit_step_packed(i, t16):
        cand = t16 + lax.shift_left(jnp.int32(1), jnp.int32(15) - i)
        cand_f = lax.bitcast_convert_type(key16_to_f32_bits(cand), f32)
        cand_b = jnp.broadcast_to(cand_f[0:1, :].astype(xb.dtype), (PACKED_ROWS, tq))
        return jnp.where(count_ge_packed(cand_b) >= kf, cand, t16)

    t16 = lax.fori_loop(0, 16, bit_step_packed, jnp.full((SUBLANES, tq), -2 ** 15, jnp.int32))

    c0_bits = key16_to_f32_bits(t16)
    c0_key = jnp.where(c0_bits >= 0, c0_bits, c0_bits ^ jnp.int32(0x7FFFFFFF))
    has_k = t16 > KEY16_NEG_INF

    def bit_step(i, carry):
        t_key, t_cnt = carry
        cand = t_key + lax.shift_left(jnp.int32(1), jnp.int32(16) - i)
        cnt = count_ge(_key_to_float(cand))
        ok = cnt >= kf
        return jnp.where(ok, cand, t_key), jnp.where(ok, cnt, t_cnt)

    t_key, t_cnt = lax.fori_loop(0, 17, bit_step, (c0_key - 2 ** 16, jnp.full((SUBLANES, tq), kf, f32)))
    t_key = jnp.where(has_k, t_key, INT_MIN)

    t_sel = _key_to_float(jnp.maximum(t_key, KEY_NEG_INF + 1))[0:1, :]

    def bias_chunk(c, rows, _):
        bias[c, 0:rows, :] = jnp.where(xs[c, 0:rows, :] >= t_sel, 0.0, NEG_BIG).astype(bias.dtype)
        return 0

    over_chunks(bias_chunk, 0)

    real = t_key > KEY_NEG_INF
    tied = (t_cnt > kf) & real

    @pl.when(jnp.max(jnp.where(tied, 1.0, 0.0)) > 0.0)
    def _():
        lo0 = _key_to_float(t_key)
        hi0 = _key_to_float(t_key + 1)
        n_hi0 = count_ge(hi0)
        n_eq = count_where(lambda x: x == lo0)
        unequal = tied & (n_eq != t_cnt - n_hi0)

        def refined(args):
            def refine(_, carry):
                lo, hi, n_hi = carry
                mid = lo + (hi - lo) * 0.5
                cnt = count_ge(mid)
                ok = cnt >= kf
                return jnp.where(ok, mid, lo), jnp.where(ok, hi, mid), jnp.where(ok, n_hi, cnt)
            return lax.fori_loop(0, N_REFINE, refine, args)

        lo8, hi8, n_hi8 = lax.cond(jnp.max(jnp.where(unequal, 1.0, 0.0)) > 0.0, refined, lambda args: args,
                                   (lo0, hi0, n_hi0))
        need8 = kf - n_hi8
        t_sel8 = _key_to_float(jnp.maximum(t_key, KEY_NEG_INF + 1))

        def tie_chunk(c, rows, seen8):
            slabs = (rows // SUBLANES, SUBLANES, tq)
            x = xs[c, 0:rows, :].reshape(slabs)
            above = jnp.where(x >= hi8, 1.0, 0.0)
            tie = jnp.where(x >= lo8, 1.0, 0.0) - above
            before = _dot(earlier[0:rows, 0:rows], tie.reshape(rows, tq).astype(bf16)).reshape(slabs)
            sel = above + jnp.where(before + seen8 < need8, tie, 0.0)
            sel = jnp.where(real, sel, jnp.where(x >= t_sel8, 1.0, 0.0))
            bias[c, 0:rows, :] = ((sel - 1.0) * -NEG_BIG).reshape(rows, tq).astype(bias.dtype)
            total = jnp.sum(jnp.sum(tie, axis=0), axis=0, keepdims=True)
            return seen8 + jnp.broadcast_to(total, (SUBLANES, tq))

        over_chunks(tie_chunk, jnp.zeros((SUBLANES, tq), f32))

    m_ref[...] = jnp.full(m_ref.shape, -jnp.inf, f32)
    acc_ref[...] = jnp.zeros(acc_ref.shape, f32)

    def attend_chunk(c, rows, _):
        b = bias[c, 0:rows, :]
        for h in range(DSA_HEADS):
            hs = slice(h * tq, (h + 1) * tq)
            s = logits[c, 0:rows, hs] + b
            m_old = m_ref[:, hs]
            m_new = jnp.maximum(m_old, jnp.max(s, axis=0, keepdims=True).astype(f32))
            alpha_ref[:, hs] = jnp.exp2(m_old - m_new)
            p_ref[0:rows, hs] = jnp.exp2(s - m_new.astype(bf16))
            m_ref[:, hs] = m_new
        acc_ref[...] = alpha_ref[...] * acc_ref[...] + _dot(vaug[c, :, 0:rows], p_ref[0:rows, :])
        return 0

    over_chunks(attend_chunk, 0)

    for p in range(N_PAIRS):
        halves = []
        for e in range(PAIR):
            hs = slice((PAIR * p + e) * tq, (PAIR * p + e + 1) * tq)
            halves.append(acc_ref[0:HEAD_DIM, hs] * (1.0 / acc_ref[HEAD_DIM:HEAD_DIM + 1, hs]))
        out_ref[:, p * LANES:(p + 1) * LANES] = jnp.concatenate(halves, axis=0).T.astype(out_ref.dtype)


def _dsa(aq, iq, iw, kk, vv, ik, batch, seq):
    t = aq.shape[0]
    tq = min(TQ, seq)
    kc = min(KC, seq)
    assert kc in (tq, 2 * tq), "a query tile must end on a whole or a half key chunk"
    nq = seq // tq
    k_sel = min(TOPK_MAX, seq // 4)
    heads = DSA_HEADS
    qrow = lambda b, j: (b * nq + j, 0)
    brow = lambda b, j: (b, 0)
    return pl.pallas_call(
        functools.partial(_dsa_kernel, k_sel),
        grid=(batch, nq),
        in_specs=[pl.BlockSpec((tq, GROUP), qrow), pl.BlockSpec((tq, GROUP), qrow),
                  pl.BlockSpec((tq, LANES), qrow),
                  pl.BlockSpec((seq, LANES), brow), pl.BlockSpec((seq, LANES), brow),
                  pl.BlockSpec((seq, LANES), brow)],
        out_specs=pl.BlockSpec((tq, GROUP), qrow),
        out_shape=jax.ShapeDtypeStruct((t, GROUP), bf16),
        scratch_shapes=[
            pltpu.VMEM((heads * tq, LANES), bf16),
            pltpu.VMEM((heads * tq, LANES), bf16),
            pltpu.VMEM((seq // kc, kc, tq), f32),
            pltpu.VMEM((seq // kc, kc, tq), bf16),
            pltpu.VMEM((seq // kc, kc, tq), bf16),
            pltpu.VMEM((kc, heads * tq), f32),
            pltpu.VMEM((seq // kc, kc, heads * tq), bf16),
            pltpu.VMEM((kc, heads * tq), bf16),
            pltpu.VMEM((1, heads * tq), f32),
            pltpu.VMEM((1, heads * tq), f32),
            pltpu.VMEM((HEAD_DIM + SUM_ROWS, heads * tq), f32),
            pltpu.VMEM((seq // kc, HEAD_DIM + SUM_ROWS, kc), bf16),
            pltpu.VMEM((kc, kc), bf16),
        ],
        compiler_params=pltpu.CompilerParams(dimension_semantics=("arbitrary", "arbitrary"),
                                             vmem_limit_bytes=VMEM_LIMIT),
        name="dsa",
    )(aq, iq, iw, kk, vv, ik)


def _mix_ffn_kernel(ret_ref, att_ref, x_ref, wo_ref, gmix_ref, gpre_ref, wg_ref, wu_ref, cwg_ref, cwu_ref,
                    cbg_ref, cbu_ref, wd_ref, gpost_ref, out_ref, carry_ref):
    tm = x_ref.shape[0]

    @pl.when(pl.program_id(1) == 0)
    def _():
        carry_ref[...] = jnp.zeros_like(carry_ref)

    mixed = _dot(ret_ref[...], wo_ref[0:GROUP, :]) + _dot(att_ref[...], wo_ref[GROUP:2 * GROUP, :])
    h = x_ref[...] + _rms(mixed, gmix_ref[...])

    hn = _rms(h, gpre_ref[...]).astype(bf16)
    row = lax.broadcasted_iota(jnp.int32, (tm, wg_ref.shape[1]), 0)

    def conv(part, w_ref, b_ref, up):
        tail = carry_ref[part]
        carry_ref[part] = up[tm - SUBLANES:tm, :]
        m1 = jnp.where(row == 0, tail[7:8, :], pltpu.roll(up, 1, 0))
        m2 = jnp.where(row == 0, tail[6:7, :], jnp.where(row == 1, tail[7:8, :], pltpu.roll(up, 2, 0)))
        return b_ref[...] + m2 * w_ref[0:1, :] + m1 * w_ref[1:2, :] + up * w_ref[2:3, :]

    g = conv(0, cwg_ref, cbg_ref, _dot(hn, wg_ref[...]))
    u = conv(1, cwu_ref, cbu_ref, _dot(hn, wu_ref[...]))
    ffn = _dot((_silu(g) * u).astype(bf16), wd_ref[...])
    out_ref[...] = h + _rms(ffn, gpost_ref[...])


def _mix_ffn(ret, att, x2, w_out, g_mix_post, g_ffn_pre, w_up, conv_w, conv_b, w_down, g_ffn_post, batch, seq):
    t, d = x2.shape
    tm = min(TM_FFN, seq)
    n_seq = seq // tm
    once = pl.Buffered(1)
    row = lambda b, i: (b * n_seq + i, 0)
    fixed = lambda b, i: (0, 0)
    second = lambda b, i: (0, 1)
    return pl.pallas_call(
        _mix_ffn_kernel,
        grid=(batch, n_seq),
        in_specs=[pl.BlockSpec((tm, GROUP), row), pl.BlockSpec((tm, GROUP), row), pl.BlockSpec((tm, d), row),
                  pl.BlockSpec((2 * GROUP, d), fixed, pipeline_mode=once),
                  pl.BlockSpec((1, d), fixed), pl.BlockSpec((1, d), fixed),
                  pl.BlockSpec((d, D_FF), fixed, pipeline_mode=once),
                  pl.BlockSpec((d, D_FF), second, pipeline_mode=once),
                  pl.BlockSpec((CONV_WIDTH, D_FF), fixed), pl.BlockSpec((CONV_WIDTH, D_FF), second),
                  pl.BlockSpec((1, D_FF), fixed), pl.BlockSpec((1, D_FF), second),
                  pl.BlockSpec((D_FF, d), fixed, pipeline_mode=once),
                  pl.BlockSpec((1, d), fixed)],
        out_specs=pl.BlockSpec((tm, d), row),
        out_shape=jax.ShapeDtypeStruct((t, d), f32),
        scratch_shapes=[pltpu.VMEM((2, SUBLANES, D_FF), f32)],
        compiler_params=pltpu.CompilerParams(dimension_semantics=("arbitrary", "arbitrary"),
                                             vmem_limit_bytes=VMEM_LIMIT),
        name="mix_ffn",
    )(ret, att, x2, w_out, g_mix_post, g_ffn_pre, w_up, w_up, conv_w, conv_w, conv_b, conv_b, w_down, g_ffn_post)


def _relayout_w_in(w):
    d = w.shape[0]
    w = w.astype(bf16)
    sizes = (GROUP, GROUP, GROUP, GROUP, GROUP, HEAD_DIM, HEAD_DIM, IDX_HEADS * IDX_DIM, IDX_DIM, IDX_HEADS)
    parts, off = [], 0
    for n in sizes:
        parts.append(w[:, off:off + n])
        off += n
    rq, rk, rv, rg, aq, ak, av, iq, ik, iw = parts
    pad = jnp.zeros((d, LANES - IDX_HEADS), w.dtype)
    return jnp.concatenate([rq, rk, aq, iq, rv, rg, ak, ak, av, av, ik, ik, iw, pad], axis=1)


def _rope_tables(seq):
    half = HEAD_DIM // 2
    inv = ROPE_THETA ** (-jnp.arange(0, HEAD_DIM, 2, dtype=f32) / HEAD_DIM)
    ang = jnp.arange(seq, dtype=f32)[:, None] * inv[None, :]
    cos, sin = jnp.cos(ang), jnp.sin(ang)
    cos_t = jnp.tile(cos, (1, LANES // half))
    sin_t = jnp.tile(jnp.concatenate([-sin, sin], axis=1), (1, PAIR))
    return cos_t, sin_t


def kernel(x, mix_norm_pre, mix_norm_post, w_in, w_out, ffn_norm_pre, ffn_norm_post,
           w_up, conv_w, conv_b, w_down):
    batch, seq, d = x.shape
    depth = w_in.shape[0]
    cos_t, sin_t = _rope_tables(seq)
    h = x.reshape(batch * seq, d)
    for l in range(depth):
        rq, rk, aq, iq, rv, rg, kk, vv, ik, iw = _inproj(
            h, mix_norm_pre[l][None, :], _relayout_w_in(w_in[l]), cos_t, sin_t, seq)
        ret = _retention(rq, rk, rv, rg, batch, seq)
        att = _dsa(aq, iq, iw, kk, vv, ik, batch, seq)
        h = _mix_ffn(ret, att, h, w_out[l].astype(bf16), mix_norm_post[l][None, :], ffn_norm_pre[l][None, :],
                     w_up[l].astype(bf16), conv_w[l], conv_b[l][None, :], w_down[l].astype(bf16),
                     ffn_norm_post[l][None, :], batch, seq)
    return h.reshape(batch, seq, d)
```

```python
import functools

import jax
import jax.numpy as jnp
from jax import lax
from jax.experimental import pallas as pl
from jax.experimental.pallas import tpu as pltpu

HEAD_DIM = 64
RET_HEADS = 8
DSA_HEADS = 8
IDX_HEADS = 8
IDX_DIM = 64
TOPK_MAX = 256
D_FF = 2816
CONV_WIDTH = 3
ROPE_THETA = 10000.0
EPS = 1e-6

LANES = 128
SUBLANES = 8
PAIR = LANES // HEAD_DIM
N_PAIRS = RET_HEADS // PAIR
GROUP = RET_HEADS * HEAD_DIM

_OFF_RQ, _OFF_RK, _OFF_AQ, _OFF_IQ, _OFF_RV, _OFF_RG = (i * GROUP for i in range(6))
_OFF_KK = 6 * GROUP
_OFF_VV = _OFF_KK + LANES
_OFF_IK = _OFF_VV + LANES
_OFF_IW = _OFF_IK + LANES
_W_TOTAL = _OFF_IW + LANES

TM_PROJ = 1024
RET_CHUNK = 256
TQ = 256
KC = 512
TM_FFN = 512
V7X_VMEM_BYTES = 64 * 1024 * 1024
VMEM_LIMIT = V7X_VMEM_BYTES * 3 // 4

NEG_BIG = -1e30
INT_MIN = -2 ** 31
KEY_NEG_INF = -2139095041
KEY16_NEG_INF = -32641
PACKED_ROWS = 16
N_COUNT_ACC = 4
N_REFINE = 12
SUM_ROWS = 16
LOG2E = 1.4426950408889634

f32 = jnp.float32
bf16 = jnp.bfloat16


def _dot(a, b):
    return jnp.dot(a, b, preferred_element_type=f32)


def _dot_nt(a, b):
    return lax.dot_general(a, b, (((1,), (1,)), ((), ())), preferred_element_type=f32)


def _dot_tn(a, b):
    return lax.dot_general(a, b, (((0,), (0,)), ((), ())), preferred_element_type=f32)


def _rms(x, g):
    return x * lax.rsqrt(jnp.mean(x * x, axis=-1, keepdims=True) + EPS) * g


def _silu(x):
    return x * (1.0 / (1.0 + jnp.exp(-x)))


def _inproj_kernel(x_ref, g_ref, w_ref, ch_ref, sh_ref,
                   rq_ref, rk_ref, aq_ref, iq_ref, rv_ref, rg_ref, kk_ref, vv_ref, ik_ref, iw_ref):
    tm = x_ref.shape[0]
    hn = _rms(x_ref[...], g_ref[...]).astype(bf16)
    cos = ch_ref[...]
    sin = sh_ref[...]
    lane = lax.broadcasted_iota(jnp.int32, (tm, LANES), 1)
    first_half = (lane % HEAD_DIM) < (HEAD_DIM // 2)

    def proj(off, width):
        return _dot(hn, w_ref[:, off:off + width])

    def rope(p):
        swapped = jnp.where(first_half, pltpu.roll(p, LANES - HEAD_DIM // 2, 1),
                            pltpu.roll(p, HEAD_DIM // 2, 1))
        return p * cos + swapped * sin

    def roped_group(off, out_ref, scale):
        p = proj(off, GROUP)
        for b in range(GROUP // LANES):
            r = rope(p[:, b * LANES:(b + 1) * LANES])
            if scale != 1.0:
                r = r * scale
            out_ref[:, b * LANES:(b + 1) * LANES] = r.astype(out_ref.dtype)

    roped_group(_OFF_RQ, rq_ref, 1.0)
    roped_group(_OFF_RK, rk_ref, HEAD_DIM ** -0.5)
    roped_group(_OFF_AQ, aq_ref, HEAD_DIM ** -0.5 * LOG2E)
    roped_group(_OFF_IQ, iq_ref, 1.0)
    rv_ref[...] = proj(_OFF_RV, GROUP).astype(bf16)
    rg_ref[...] = proj(_OFF_RG, GROUP)
    small = proj(_OFF_KK, 4 * LANES)
    kk_ref[...] = rope(small[:, 0:LANES]).astype(bf16)
    vv_ref[...] = small[:, LANES:2 * LANES].astype(bf16)
    ik_ref[...] = rope(small[:, 2 * LANES:3 * LANES]).astype(bf16)
    iw_ref[...] = small[:, 3 * LANES:4 * LANES] * (IDX_HEADS ** -0.5 * IDX_DIM ** -0.5)


def _inproj(x2, g, w, cos_t, sin_t, seq):
    t, d = x2.shape
    tm = min(TM_PROJ, seq)
    n_seq = seq // tm
    row = lambda i: (i, 0)
    fixed = lambda i: (0, 0)
    pos = lambda i: (i % n_seq, 0)
    wide = lambda dt: jax.ShapeDtypeStruct((t, GROUP), dt)
    narrow = lambda dt: jax.ShapeDtypeStruct((t, LANES), dt)
    return pl.pallas_call(
        _inproj_kernel,
        grid=(t // tm,),
        in_specs=[pl.BlockSpec((tm, d), row), pl.BlockSpec((1, d), fixed),
                  pl.BlockSpec((d, _W_TOTAL), fixed, pipeline_mode=pl.Buffered(1)),
                  pl.BlockSpec((tm, LANES), pos), pl.BlockSpec((tm, LANES), pos)],
        out_specs=[pl.BlockSpec((tm, GROUP), row)] * 6 + [pl.BlockSpec((tm, LANES), row)] * 4,
        out_shape=[wide(bf16), wide(bf16), wide(bf16), wide(bf16), wide(bf16), wide(f32),
                   narrow(bf16), narrow(bf16), narrow(bf16), narrow(f32)],
        compiler_params=pltpu.CompilerParams(dimension_semantics=("arbitrary",),
                                             vmem_limit_bytes=VMEM_LIMIT),
        name="in_proj",
    )(x2, g, w, cos_t, sin_t)


def _retention_kernel(rq_ref, rk_ref, rv_ref, rg_ref, decay_ref, zeta_ref, xi_ref, cd_ref,
                      out_ref, state_ref):
    c = rq_ref.shape[0]

    @pl.when(pl.program_id(1) == 0)
    def _():
        state_ref[...] = jnp.zeros_like(state_ref)

    lane = lax.broadcasted_iota(jnp.int32, (c, LANES), 1)
    even = lane < HEAD_DIM
    r_i = lax.broadcasted_iota(jnp.int32, (LANES, LANES), 0)
    c_i = lax.broadcasted_iota(jnp.int32, (LANES, LANES), 1)
    same_head = (r_i < HEAD_DIM) == (c_i < HEAD_DIM)
    ones_bd = jnp.where(same_head, 1.0, 0.0).astype(bf16)

    for p in range(N_PAIRS):
        sl = slice(p * LANES, (p + 1) * LANES)
        q2 = rq_ref[:, sl]
        k2 = rk_ref[:, sl]
        v2 = rv_ref[:, sl]
        zero = jnp.zeros_like(q2)
        q_eo = jnp.concatenate([jnp.where(even, q2, zero), jnp.where(even, zero, q2)], axis=0)
        s_eo = _dot_nt(q_eo, k2) * decay_ref[2 * p:2 * p + 2].reshape(2 * c, c)
        pv = _dot(s_eo.astype(bf16), v2)
        inner = jnp.where(even, pv[0:c, :], pv[c:2 * c, :])
        r_prev = state_ref[p]
        cross = _dot(q2, r_prev.astype(bf16)) * xi_ref[p]
        o = inner + cross
        kz = (k2.astype(f32) * zeta_ref[p]).astype(bf16)
        s_new = jnp.where(same_head, _dot_tn(kz, v2), 0.0)
        state_ref[p] = cd_ref[p] * r_prev + s_new
        sq = o * o
        hi = sq.astype(bf16)
        lo = (sq - hi.astype(f32)).astype(bf16)
        sums = _dot(jnp.concatenate([hi, lo], axis=0), ones_bd)
        ms = (sums[0:c, :] + sums[c:2 * c, :]) * (1.0 / HEAD_DIM)
        o = o * lax.rsqrt(ms + EPS)
        out_ref[:, sl] = (o * _silu(rg_ref[:, sl])).astype(out_ref.dtype)


def _retention_tables(c):
    h = jnp.arange(RET_HEADS, dtype=f32)
    log_g = jnp.log(1.0 - 2.0 ** (-5.0 - h))
    pos = jnp.arange(c, dtype=f32)
    diff = pos[:, None] - pos[None, :]
    decay = jnp.where(diff >= 0, jnp.exp(log_g[:, None, None] * jnp.maximum(diff, 0.0)), 0.0)
    zeta = jnp.exp(log_g[:, None] * (c - 1.0 - pos)[None, :])
    xi = jnp.exp(log_g[:, None] * (pos + 1.0)[None, :])
    cd = jnp.exp(log_g * c)

    def pair_lanes(a):
        a = a.reshape(N_PAIRS, PAIR, c)
        return jnp.repeat(jnp.transpose(a, (0, 2, 1)), HEAD_DIM, axis=2)

    cd2 = jnp.repeat(cd.reshape(N_PAIRS, 1, PAIR), HEAD_DIM, axis=2)
    return decay, pair_lanes(zeta), pair_lanes(xi), cd2


def _retention(rq, rk, rv, rg, batch, seq):
    t = rq.shape[0]
    c = min(RET_CHUNK, seq)
    n = seq // c
    decay, zeta, xi, cd = _retention_tables(c)
    row = lambda b, i: (b * n + i, 0)
    fix3 = lambda b, i: (0, 0, 0)
    return pl.pallas_call(
        _retention_kernel,
        grid=(batch, n),
        in_specs=[pl.BlockSpec((c, GROUP), row)] * 4 + [
            pl.BlockSpec((RET_HEADS, c, c), fix3), pl.BlockSpec((N_PAIRS, c, LANES), fix3),
            pl.BlockSpec((N_PAIRS, c, LANES), fix3), pl.BlockSpec((N_PAIRS, 1, LANES), fix3)],
        out_specs=pl.BlockSpec((c, GROUP), row),
        out_shape=jax.ShapeDtypeStruct((t, GROUP), bf16),
        scratch_shapes=[pltpu.VMEM((N_PAIRS, LANES, LANES), f32)],
        compiler_params=pltpu.CompilerParams(dimension_semantics=("arbitrary", "arbitrary"),
                                             vmem_limit_bytes=VMEM_LIMIT),
        name="retention",
    )(rq, rk, rv, rg, decay, zeta, xi, cd)


def _key_to_float(key):
    bits = jnp.where(key >= 0, key, key ^ jnp.int32(0x7FFFFFFF))
    return lax.bitcast_convert_type(bits, f32)


def _dsa_kernel(k_sel, aq_ref, iq_ref, iw_ref, kk_ref, vv_ref, ik_ref, out_ref,
                lhs_a, lhs_i, xs, xb, bias, prod, p_ref, m_ref, alpha_ref, acc_ref, vaug, earlier):
    tq = aq_ref.shape[0]
    n_kc, kc, _ = xs.shape
    j = pl.program_id(1)
    half = kc // 2
    n_keys = (j + 1) * tq
    n_full = n_keys // kc
    has_half = n_keys % kc != 0
    kf = float(k_sel)

    def over_chunks(body, init):
        carry = lax.fori_loop(0, n_full, lambda c, a: body(c, kc, a), init)
        return lax.cond(has_half, lambda a: body(n_full, half, a), lambda a: a, carry)

    @pl.when((pl.program_id(0) == 0) & (j == 0))
    def _():
        r_i = lax.broadcasted_iota(jnp.int32, (kc, kc), 0)
        c_i = lax.broadcasted_iota(jnp.int32, (kc, kc), 1)
        earlier[...] = jnp.where(c_i < r_i, 1.0, 0.0).astype(bf16)

    @pl.when(j == 0)
    def _():
        row = lax.broadcasted_iota(jnp.int32, (LANES, kc), 0)
        for c in range(n_kc):
            vt = vv_ref[c * kc:(c + 1) * kc, :].astype(f32).T
            vaug[c] = jnp.where(row < HEAD_DIM, vt, 1.0)[0:HEAD_DIM + SUM_ROWS, :].astype(bf16)

    lane = lax.broadcasted_iota(jnp.int32, (tq, LANES), 1)
    even = lane < HEAD_DIM
    for p in range(N_PAIRS):
        sl = slice(p * LANES, (p + 1) * LANES)
        a2 = aq_ref[:, sl]
        i2 = iq_ref[:, sl]
        zero = jnp.zeros_like(a2)
        lhs_a[(2 * p) * tq:(2 * p + 1) * tq, :] = jnp.where(even, a2, zero)
        lhs_a[(2 * p + 1) * tq:(2 * p + 2) * tq, :] = jnp.where(even, zero, a2)
        lhs_i[(2 * p) * tq:(2 * p + 1) * tq, :] = jnp.where(even, i2, zero)
        lhs_i[(2 * p + 1) * tq:(2 * p + 2) * tq, :] = jnp.where(even, zero, i2)
    w_heads = iw_ref[...].T[0:IDX_HEADS, :]

    def for_key_chunks(key_ref, lhs_ref, work):
        def one_chunk(c, rows, _):
            start = pl.multiple_of(c * kc, half)
            prod[0:rows, :] = _dot_nt(key_ref[pl.ds(start, rows), :], lhs_ref[...])
            work(c, rows)
            return 0

        over_chunks(one_chunk, 0)

    def score_chunk(c, rows):
        isc = jnp.zeros((rows, tq), f32)
        for h in range(IDX_HEADS):
            isc = isc + w_heads[h:h + 1, :] * jnp.maximum(prod[0:rows, h * tq:(h + 1) * tq], 0.0)
        key_pos = c * kc + lax.broadcasted_iota(jnp.int32, (rows, tq), 0)
        q_pos = n_keys - tq + lax.broadcasted_iota(jnp.int32, (rows, tq), 1)
        isc = jnp.where(key_pos <= q_pos, isc, -jnp.inf)
        xs[c, 0:rows, :] = isc
        xb[c, 0:rows, :] = isc.astype(xb.dtype)

    for_key_chunks(ik_ref, lhs_i, score_chunk)

    def count_where(pred):
        def body(c, rows, accs):
            x = xs[c, 0:rows, :]
            accs = list(accs)
            for r in range(rows // SUBLANES):
                hit = jnp.where(pred(x[r * SUBLANES:(r + 1) * SUBLANES, :]), 1.0, 0.0)
                accs[r % N_COUNT_ACC] = accs[r % N_COUNT_ACC] + hit
            return tuple(accs)
        zero = jnp.zeros((SUBLANES, tq), f32)
        accs = over_chunks(body, (zero,) * N_COUNT_ACC)
        total = functools.reduce(lambda a, b: a + b, accs)
        return jnp.broadcast_to(jnp.sum(total, axis=0, keepdims=True), (SUBLANES, tq))

    def count_ge(cand_f):
        return count_where(lambda x: x >= cand_f)

    def count_ge_packed(cand_b):
        one = jnp.ones((PACKED_ROWS, tq), xb.dtype)
        zero = jnp.zeros((PACKED_ROWS, tq), xb.dtype)

        def body(c, rows, accs):
            x = xb[c, 0:rows, :]
            accs = list(accs)
            for r in range(rows // PACKED_ROWS):
                hit = jnp.where(x[r * PACKED_ROWS:(r + 1) * PACKED_ROWS, :] >= cand_b, one, zero)
                accs[r % N_COUNT_ACC] = accs[r % N_COUNT_ACC] + hit
            return tuple(accs)
        accs = over_chunks(body, (zero,) * N_COUNT_ACC)
        total = functools.reduce(lambda a, b: a + b, [a.astype(f32) for a in accs])
        return jnp.broadcast_to(jnp.sum(total, axis=0, keepdims=True), (SUBLANES, tq))

    def key16_to_f32_bits(key16):
        b16 = jnp.where(key16 >= 0, key16, key16 ^ jnp.int32(0x7FFF)) & jnp.int32(0xFFFF)
        return lax.shift_left(b16, jnp.int32(16))

    def bit_step_packed(i, t16):
        cand = t16 + lax.shift_left(jnp.int32(1), jnp.int32(15) - i)
        cand_f = lax.bitcast_convert_type(key16_to_f32_bits(cand), f32)
        cand_b = jnp.broadcast_to(cand_f[0:1, :].astype(xb.dtype), (PACKED_ROWS, tq))
        return jnp.where(count_ge_packed(cand_b) >= kf, cand, t16)

    t16 = lax.fori_loop(0, 16, bit_step_packed, jnp.full((SUBLANES, tq), -2 ** 15, jnp.int32))

    c0_bits = key16_to_f32_bits(t16)
    c0_key = jnp.where(c0_bits >= 0, c0_bits, c0_bits ^ jnp.int32(0x7FFFFFFF))
    has_k = t16 > KEY16_NEG_INF

    def bit_step(i, carry):
        t_key, t_cnt = carry
        cand = t_key + lax.shift_left(jnp.int32(1), jnp.int32(16) - i)
        cnt = count_ge(_key_to_float(cand))
        ok = cnt >= kf
        return jnp.where(ok, cand, t_key), jnp.where(ok, cnt, t_cnt)

    t_key, t_cnt = lax.fori_loop(0, 17, bit_step, (c0_key - 2 ** 16, jnp.full((SUBLANES, tq), kf, f32)))
    t_key = jnp.where(has_k, t_key, INT_MIN)

    t_sel = _key_to_float(jnp.maximum(t_key, KEY_NEG_INF + 1))[0:1, :]

    def bias_chunk(c, rows, _):
        bias[c, 0:rows, :] = jnp.where(xs[c, 0:rows, :] >= t_sel, 0.0, NEG_BIG).astype(bias.dtype)
        return 0

    over_chunks(bias_chunk, 0)

    real = t_key > KEY_NEG_INF
    tied = (t_cnt > kf) & real

    @pl.when(jnp.max(jnp.where(tied, 1.0, 0.0)) > 0.0)
    def _():
        lo0 = _key_to_float(t_key)
        hi0 = _key_to_float(t_key + 1)
        n_hi0 = count_ge(hi0)
        n_eq = count_where(lambda x: x == lo0)
        unequal = tied & (n_eq != t_cnt - n_hi0)

        def refined(args):
            def refine(_, carry):
                lo, hi, n_hi = carry
                mid = lo + (hi - lo) * 0.5
                cnt = count_ge(mid)
                ok = cnt >= kf
                return jnp.where(ok, mid, lo), jnp.where(ok, hi, mid), jnp.where(ok, n_hi, cnt)
            return lax.fori_loop(0, N_REFINE, refine, args)

        lo8, hi8, n_hi8 = lax.cond(jnp.max(jnp.where(unequal, 1.0, 0.0)) > 0.0, refined, lambda args: args,
                                   (lo0, hi0, n_hi0))
        need8 = kf - n_hi8
        t_sel8 = _key_to_float(jnp.maximum(t_key, KEY_NEG_INF + 1))

        def tie_chunk(c, rows, seen8):
            slabs = (rows // SUBLANES, SUBLANES, tq)
            x = xs[c, 0:rows, :].reshape(slabs)
            above = jnp.where(x >= hi8, 1.0, 0.0)
            tie = jnp.where(x >= lo8, 1.0, 0.0) - above
            before = _dot(earlier[0:rows, 0:rows], tie.reshape(rows, tq).astype(bf16)).reshape(slabs)
            sel = above + jnp.where(before + seen8 < need8, tie, 0.0)
            sel = jnp.where(real, sel, jnp.where(x >= t_sel8, 1.0, 0.0))
            bias[c, 0:rows, :] = ((sel - 1.0) * -NEG_BIG).reshape(rows, tq).astype(bias.dtype)
            total = jnp.sum(jnp.sum(tie, axis=0), axis=0, keepdims=True)
            return seen8 + jnp.broadcast_to(total, (SUBLANES, tq))

        over_chunks(tie_chunk, jnp.zeros((SUBLANES, tq), f32))

    m_ref[...] = jnp.full(m_ref.shape, -jnp.inf, f32)
    acc_ref[...] = jnp.zeros(acc_ref.shape, f32)

    def attend_chunk(c, rows):
        b = bias[c, 0:rows, :]
        for h in range(DSA_HEADS):
            hs = slice(h * tq, (h + 1) * tq)
            s = prod[0:rows, hs].astype(bf16) + b
            m_old = m_ref[:, hs]
            m_new = jnp.maximum(m_old, jnp.max(s, axis=0, keepdims=True).astype(f32))
            alpha_ref[:, hs] = jnp.exp2(m_old - m_new)
            p_ref[0:rows, hs] = jnp.exp2(s - m_new.astype(bf16))
            m_ref[:, hs] = m_new
        acc_ref[...] = alpha_ref[...] * acc_ref[...] + _dot(vaug[c, :, 0:rows], p_ref[0:rows, :])

    for_key_chunks(kk_ref, lhs_a, attend_chunk)

    for p in range(N_PAIRS):
        halves = []
        for e in range(PAIR):
            hs = slice((PAIR * p + e) * tq, (PAIR * p + e + 1) * tq)
            halves.append(acc_ref[0:HEAD_DIM, hs] * (1.0 / acc_ref[HEAD_DIM:HEAD_DIM + 1, hs]))
        out_ref[:, p * LANES:(p + 1) * LANES] = jnp.concatenate(halves, axis=0).T.astype(out_ref.dtype)


def _dsa(aq, iq, iw, kk, vv, ik, batch, seq):
    t = aq.shape[0]
    tq = min(TQ, seq)
    kc = min(KC, seq)
    assert kc in (tq, 2 * tq), "a query tile must end on a whole or a half key chunk"
    nq = seq // tq
    k_sel = min(TOPK_MAX, seq // 4)
    heads = DSA_HEADS
    qrow = lambda b, j: (b * nq + j, 0)
    brow = lambda b, j: (b, 0)
    return pl.pallas_call(
        functools.partial(_dsa_kernel, k_sel),
        grid=(batch, nq),
        in_specs=[pl.BlockSpec((tq, GROUP), qrow), pl.BlockSpec((tq, GROUP), qrow),
                  pl.BlockSpec((tq, LANES), qrow),
                  pl.BlockSpec((seq, LANES), brow), pl.BlockSpec((seq, LANES), brow),
                  pl.BlockSpec((seq, LANES), brow)],
        out_specs=pl.BlockSpec((tq, GROUP), qrow),
        out_shape=jax.ShapeDtypeStruct((t, GROUP), bf16),
        scratch_shapes=[
            pltpu.VMEM((heads * tq, LANES), bf16),
            pltpu.VMEM((heads * tq, LANES), bf16),
            pltpu.VMEM((seq // kc, kc, tq), f32),
            pltpu.VMEM((seq // kc, kc, tq), bf16),
            pltpu.VMEM((seq // kc, kc, tq), bf16),
            pltpu.VMEM((kc, heads * tq), f32),
            pltpu.VMEM((kc, heads * tq), bf16),
            pltpu.VMEM((1, heads * tq), f32),
            pltpu.VMEM((1, heads * tq), f32),
            pltpu.VMEM((HEAD_DIM + SUM_ROWS, heads * tq), f32),
            pltpu.VMEM((seq // kc, HEAD_DIM + SUM_ROWS, kc), bf16),
            pltpu.VMEM((kc, kc), bf16),
        ],
        compiler_params=pltpu.CompilerParams(dimension_semantics=("arbitrary", "arbitrary"),
                                             vmem_limit_bytes=VMEM_LIMIT),
        name="dsa",
    )(aq, iq, iw, kk, vv, ik)


def _mix_ffn_kernel(ret_ref, att_ref, x_ref, wo_ref, gmix_ref, gpre_ref, wg_ref, wu_ref, cwg_ref, cwu_ref,
                    cbg_ref, cbu_ref, wd_ref, gpost_ref, out_ref, carry_ref):
    tm = x_ref.shape[0]

    @pl.when(pl.program_id(1) == 0)
    def _():
        carry_ref[...] = jnp.zeros_like(carry_ref)

    mixed = _dot(ret_ref[...], wo_ref[0:GROUP, :]) + _dot(att_ref[...], wo_ref[GROUP:2 * GROUP, :])
    h = x_ref[...] + _rms(mixed, gmix_ref[...])

    hn = _rms(h, gpre_ref[...]).astype(bf16)
    row = lax.broadcasted_iota(jnp.int32, (tm, wg_ref.shape[1]), 0)

    def conv(part, w_ref, b_ref, up):
        tail = carry_ref[part]
        carry_ref[part] = up[tm - SUBLANES:tm, :]
        m1 = jnp.where(row == 0, tail[7:8, :], pltpu.roll(up, 1, 0))
        m2 = jnp.where(row == 0, tail[6:7, :], jnp.where(row == 1, tail[7:8, :], pltpu.roll(up, 2, 0)))
        return b_ref[...] + m2 * w_ref[0:1, :] + m1 * w_ref[1:2, :] + up * w_ref[2:3, :]

    g = conv(0, cwg_ref, cbg_ref, _dot(hn, wg_ref[...]))
    u = conv(1, cwu_ref, cbu_ref, _dot(hn, wu_ref[...]))
    ffn = _dot((_silu(g) * u).astype(bf16), wd_ref[...])
    out_ref[...] = h + _rms(ffn, gpost_ref[...])


def _mix_ffn(ret, att, x2, w_out, g_mix_post, g_ffn_pre, w_up, conv_w, conv_b, w_down, g_ffn_post, batch, seq):
    t, d = x2.shape
    tm = min(TM_FFN, seq)
    n_seq = seq // tm
    once = pl.Buffered(1)
    row = lambda b, i: (b * n_seq + i, 0)
    fixed = lambda b, i: (0, 0)
    second = lambda b, i: (0, 1)
    return pl.pallas_call(
        _mix_ffn_kernel,
        grid=(batch, n_seq),
        in_specs=[pl.BlockSpec((tm, GROUP), row), pl.BlockSpec((tm, GROUP), row), pl.BlockSpec((tm, d), row),
                  pl.BlockSpec((2 * GROUP, d), fixed, pipeline_mode=once),
                  pl.BlockSpec((1, d), fixed), pl.BlockSpec((1, d), fixed),
                  pl.BlockSpec((d, D_FF), fixed, pipeline_mode=once),
                  pl.BlockSpec((d, D_FF), second, pipeline_mode=once),
                  pl.BlockSpec((CONV_WIDTH, D_FF), fixed), pl.BlockSpec((CONV_WIDTH, D_FF), second),
                  pl.BlockSpec((1, D_FF), fixed), pl.BlockSpec((1, D_FF), second),
                  pl.BlockSpec((D_FF, d), fixed, pipeline_mode=once),
                  pl.BlockSpec((1, d), fixed)],
        out_specs=pl.BlockSpec((tm, d), row),
        out_shape=jax.ShapeDtypeStruct((t, d), f32),
        scratch_shapes=[pltpu.VMEM((2, SUBLANES, D_FF), f32)],
        compiler_params=pltpu.CompilerParams(dimension_semantics=("arbitrary", "arbitrary"),
                                             vmem_limit_bytes=VMEM_LIMIT),
        name="mix_ffn",
    )(ret, att, x2, w_out, g_mix_post, g_ffn_pre, w_up, w_up, conv_w, conv_w, conv_b, conv_b, w_down, g_ffn_post)


def _relayout_w_in(w):
    d = w.shape[0]
    w = w.astype(bf16)
    sizes = (GROUP, GROUP, GROUP, GROUP, GROUP, HEAD_DIM, HEAD_DIM, IDX_HEADS * IDX_DIM, IDX_DIM, IDX_HEADS)
    parts, off = [], 0
    for n in sizes:
        parts.append(w[:, off:off + n])
        off += n
    rq, rk, rv, rg, aq, ak, av, iq, ik, iw = parts
    pad = jnp.zeros((d, LANES - IDX_HEADS), w.dtype)
    return jnp.concatenate([rq, rk, aq, iq, rv, rg, ak, ak, av, av, ik, ik, iw, pad], axis=1)


def _rope_tables(seq):
    half = HEAD_DIM // 2
    inv = ROPE_THETA ** (-jnp.arange(0, HEAD_DIM, 2, dtype=f32) / HEAD_DIM)
    ang = jnp.arange(seq, dtype=f32)[:, None] * inv[None, :]
    cos, sin = jnp.cos(ang), jnp.sin(ang)
    cos_t = jnp.tile(cos, (1, LANES // half))
    sin_t = jnp.tile(jnp.concatenate([-sin, sin], axis=1), (1, PAIR))
    return cos_t, sin_t


def kernel(x, mix_norm_pre, mix_norm_post, w_in, w_out, ffn_norm_pre, ffn_norm_post,
           w_up, conv_w, conv_b, w_down):
    batch, seq, d = x.shape
    depth = w_in.shape[0]
    cos_t, sin_t = _rope_tables(seq)
    h = x.reshape(batch * seq, d)
    for l in range(depth):
        rq, rk, aq, iq, rv, rg, kk, vv, ik, iw = _inproj(
            h, mix_norm_pre[l][None, :], _relayout_w_in(w_in[l]), cos_t, sin_t, seq)
        ret = _retention(rq, rk, rv, rg, batch, seq)
        att = _dsa(aq, iq, iw, kk, vv, ik, batch, seq)
        h = _mix_ffn(ret, att, h, w_out[l].astype(bf16), mix_norm_post[l][None, :], ffn_norm_pre[l][None, :],
                     w_up[l].astype(bf16), conv_w[l], conv_b[l][None, :], w_down[l].astype(bf16),
                     ffn_norm_post[l][None, :], batch, seq)
    return h.reshape(batch, seq, d)
```
